```python
import math
import jax, jax.numpy as jnp
from jax import lax
import numpy as np

D_MODEL = 1024
BATCH = 2
SEQ = 8192
DEPTH = 1
DEC_BATCH = 128
DEC_SEQ = 4
PAST_LEN = 8192
PAGE_SIZE = 128

HEAD_DIM = 64
NSA_HEADS = 8
NSA_KV_HEADS = 2
NSA_GROUP = NSA_HEADS // NSA_KV_HEADS
CMP_BLOCK = 32
CMP_STRIDE = 16
CMP_RATIO = CMP_BLOCK // CMP_STRIDE
CMP_HIDDEN = 2 * HEAD_DIM
SEL_BLOCK = 64
SEL_TOPK = 16
WINDOW = 512
DSA_HEADS = 8
DSA_KV_HEADS = 2
DSA_GROUP = DSA_HEADS // DSA_KV_HEADS
IDX_HEADS = 4
IDX_DIM = 64
DSA_TOPK_MAX = 256
N_BUCKETS = 32
MAX_DISTANCE = 128
D_FF = ((8 * D_MODEL + 3 * 256 - 1) // (3 * 256)) * 256
Q_BLOCK = 128
EPS = 1e-6

PROJ_WIDTHS = (
    NSA_HEADS * HEAD_DIM,
    2 * NSA_KV_HEADS * HEAD_DIM,
    2 * NSA_KV_HEADS * HEAD_DIM,
    2 * NSA_KV_HEADS * HEAD_DIM,
    3 * NSA_HEADS,
    DSA_HEADS * HEAD_DIM,
    2 * DSA_KV_HEADS * HEAD_DIM,
    IDX_HEADS * IDX_DIM,
    IDX_DIM,
    IDX_HEADS,
    2 * D_MODEL,
)
D_IN = sum(PROJ_WIDTHS)
SPLIT_POINTS = tuple(int(v) for v in np.cumsum(PROJ_WIDTHS)[:-1])

kernel_name = 'nsa_dsa_parallel_gated_decoder_step'


def rmsnorm(x, g):
    xf = x.astype(jnp.float32)
    y = xf * lax.rsqrt(jnp.mean(xf * xf, axis=-1, keepdims=True) + EPS)
    return (y * g.astype(jnp.float32)).astype(x.dtype)


def t5_bucket(dist):
    d = jnp.maximum(dist, 0)
    max_exact = N_BUCKETS // 2
    logd = jnp.log(jnp.maximum(d, 1).astype(jnp.float32) / max_exact) / math.log(MAX_DISTANCE / max_exact)
    large = jnp.minimum(max_exact + (logd * (N_BUCKETS - max_exact)).astype(jnp.int32), N_BUCKETS - 1)
    return jnp.where(d < max_exact, d, large)


def masked_softmax(s, mask):
    s = jnp.where(mask, s.astype(jnp.float32), -jnp.inf)
    m = jnp.max(s, axis=-1, keepdims=True)
    p = jnp.exp(s - jnp.where(jnp.isfinite(m), m, 0.0))
    return p / jnp.maximum(jnp.sum(p, axis=-1, keepdims=True), 1e-30)


def gather_pages(pool, page_table):
    g = pool[page_table]
    return g.reshape((g.shape[0], g.shape[1] * g.shape[2]) + g.shape[3:])


def project(h, w_in):
    B, T, _ = h.shape
    nq, ckv, skv, wkv, ng, dq, dkv, iq, ik, iw, mg = jnp.split(h @ w_in, list(SPLIT_POINTS), axis=-1)
    return (nq.reshape(B, T, NSA_KV_HEADS, NSA_GROUP, HEAD_DIM),
            ckv.reshape(B, T, 2, NSA_KV_HEADS, HEAD_DIM),
            skv.reshape(B, T, 2, NSA_KV_HEADS, HEAD_DIM),
            wkv.reshape(B, T, 2, NSA_KV_HEADS, HEAD_DIM),
            jax.nn.sigmoid(ng).reshape(B, T, NSA_KV_HEADS, NSA_GROUP, 3),
            dq.reshape(B, T, DSA_KV_HEADS, DSA_GROUP, HEAD_DIM),
            dkv.reshape(B, T, 2, DSA_KV_HEADS, HEAD_DIM),
            iq.reshape(B, T, IDX_HEADS, IDX_DIM),
            ik,
            iw * IDX_HEADS ** -0.5,
            jax.nn.sigmoid(mg).reshape(B, T, 2, D_MODEL))


def compress_kv(kv, cmp_pe, cmp_w1, cmp_b1, cmp_w2):
    B, L = kv.shape[:2]
    n_chunks = L // CMP_STRIDE
    n_cmp = n_chunks - CMP_RATIO + 1
    chunks = kv[:, :n_chunks * CMP_STRIDE].reshape(B, n_chunks, CMP_STRIDE, 2, kv.shape[3], HEAD_DIM)
    w1 = cmp_w1.reshape(2, CMP_RATIO, CMP_STRIDE, HEAD_DIM, CMP_HIDDEN)
    h = (cmp_b1 + jnp.einsum('cld,cldf->cf', cmp_pe, cmp_w1))[None, None, :, None, :]
    for r in range(CMP_RATIO):
        h = h + jnp.einsum('bnlchd,cldf->bnchf', chunks, w1[:, r])[:, r:r + n_cmp]
    return jnp.einsum('bnchf,cfd->bnchd', jax.nn.gelu(h), cmp_w2)


def overlap_matrix(n_cmp, n_sel):
    c0 = np.arange(n_cmp)[:, None] * CMP_STRIDE
    s0 = np.arange(n_sel)[None, :] * SEL_BLOCK
    m = (c0 < s0 + SEL_BLOCK) & (c0 + CMP_BLOCK > s0)
    return jnp.asarray(m.astype(np.float32))


def nsa_attend(q, qpos, ck, cv, sk, sv, wk, wv, wpos, gates, tab):
    B, Tq, n_kv = q.shape[:3]
    n_cmp, L = ck.shape[1], sk.shape[1]
    n_sel = -(-L // SEL_BLOCK)
    k_sel = min(SEL_TOPK, n_sel)
    scale = HEAD_DIM ** -0.5
    cend = jnp.arange(n_cmp) * CMP_STRIDE + (CMP_BLOCK - 1)
    dist = qpos[:, None] - cend[None, :]
    bias = jnp.transpose(tab[t5_bucket(dist)], (2, 3, 0, 1))
    s = jnp.einsum('bthgd,bnhd->bhgtn', q, ck) * scale + bias
    p_cmp = masked_softmax(s, dist >= 0)
    o_cmp = jnp.einsum('bhgtn,bnhd->bthgd', p_cmp.astype(cv.dtype), cv)
    imp = jnp.einsum('bhgtn,nj->bhtj', p_cmp, overlap_matrix(n_cmp, n_sel))
    blk = jnp.arange(n_sel)[None, :]
    cur = (qpos // SEL_BLOCK)[:, None]
    forced = (blk == 0) | (blk == cur) | (blk == cur - 1)
    imp = jnp.where(forced, jnp.inf, jnp.where(blk <= cur, imp, -jnp.inf))
    _, sel = lax.top_k(imp, k_sel)
    tok = (sel[..., None] * SEL_BLOCK + jnp.arange(SEL_BLOCK)).reshape(B, n_kv, Tq, k_sel * SEL_BLOCK)
    tokc = jnp.minimum(tok, L - 1)
    bi = jnp.arange(B)[:, None, None, None]
    hi = jnp.arange(n_kv)[None, :, None, None]
    ks = sk[bi, tokc, hi]
    vs = sv[bi, tokc, hi]
    dist = qpos[None, None, :, None] - tok
    bias = jnp.transpose(tab[t5_bucket(dist), hi], (0, 1, 4, 2, 3))
    s = jnp.einsum('bthgd,bhtsd->bhgts', q, ks) * scale + bias
    p = masked_softmax(s, (dist >= 0)[:, :, None])
    o_sel = jnp.einsum('bhgts,bhtsd->bthgd', p.astype(vs.dtype), vs)
    dist = qpos[:, None] - wpos[None, :]
    mask = (dist >= 0) & (dist < WINDOW) & (wpos >= 0)[None, :]
    bias = jnp.transpose(tab[t5_bucket(dist)], (2, 3, 0, 1))
    s = jnp.einsum('bthgd,bshd->bhgts', q, wk) * scale + bias
    p = masked_softmax(s, mask)
    o_win = jnp.einsum('bhgts,bshd->bthgd', p.astype(wv.dtype), wv)
    o = gates[..., 0:1] * o_cmp + gates[..., 1:2] * o_sel + gates[..., 2:3] * o_win
    return o.reshape(B, Tq, -1)


def dsa_attend(q, qpos, k, v, iq, ik, iw, k_top, tab):
    B, Tq = q.shape[:2]
    L = k.shape[1]
    scale = HEAD_DIM ** -0.5
    causal = jnp.arange(L)[None, :] <= qpos[:, None]
    score = jnp.einsum('btj,btjs->bts', iw, jax.nn.relu(jnp.einsum('btji,bsi->btjs', iq, ik)))
    score = jnp.where(causal, score.astype(jnp.float32), -jnp.inf)
    _, top = lax.top_k(score, k_top)
    bi = jnp.arange(B)[:, None, None]
    ks = k[bi, top]
    vs = v[bi, top]
    dist = qpos[None, :, None] - top
    bias = jnp.transpose(tab[t5_bucket(dist)], (0, 3, 4, 1, 2))
    s = jnp.einsum('bthgd,btshd->bhgts', q, ks) * scale + bias
    p = masked_softmax(s, (dist >= 0)[:, None, None])
    o = jnp.einsum('bhgts,btshd->bthgd', p.astype(vs.dtype), vs)
    return o.reshape(B, Tq, -1)


def merge(o_nsa, o_dsa, g, w_branch_nsa, w_branch_dsa, w_out):
    return (g[..., 0, :] * (o_nsa @ w_branch_nsa) + g[..., 1, :] * (o_dsa @ w_branch_dsa)) @ w_out


def prompt_mixer(h, w_in, cmp_pe, cmp_w1, cmp_b1, cmp_w2, tab_nsa, tab_dsa, w_branch_nsa, w_branch_dsa, w_out):
    B, T, _ = h.shape
    nq, ckv, skv, wkv, ng, dq, dkv, iq, ik, iw, mg = project(h, w_in)
    ccmp = compress_kv(ckv, cmp_pe, cmp_w1, cmp_b1, cmp_w2)
    ck, cv = ccmp[:, :, 0], ccmp[:, :, 1]
    sk, sv = skv[:, :, 0], skv[:, :, 1]
    dk, dv = dkv[:, :, 0], dkv[:, :, 1]
    win_pad = jnp.pad(wkv, ((0, 0), (WINDOW, 0), (0, 0), (0, 0), (0, 0)))
    k_top = min(DSA_TOPK_MAX, T // 4)

    def block(qb):
        t0 = qb * Q_BLOCK
        sl = lambda a: lax.dynamic_slice_in_dim(a, t0, Q_BLOCK, axis=1)
        qpos = t0 + jnp.arange(Q_BLOCK)
        w = lax.dynamic_slice_in_dim(win_pad, t0, WINDOW + Q_BLOCK, axis=1)
        wpos = t0 - WINDOW + jnp.arange(WINDOW + Q_BLOCK)
        o_n = nsa_attend(sl(nq), qpos, ck, cv, sk, sv, w[:, :, 0], w[:, :, 1], wpos, sl(ng), tab_nsa)
        o_d = dsa_attend(sl(dq), qpos, dk, dv, sl(iq), ik, sl(iw), k_top, tab_dsa)
        return merge(o_n, o_d, sl(mg), w_branch_nsa, w_branch_dsa, w_out)

    y = lax.map(block, jnp.arange(T // Q_BLOCK))
    y = jnp.moveaxis(y, 0, 1).reshape(B, T, D_MODEL)
    return y, ckv, skv, dkv, ik, wkv[:, T - min(WINDOW, T):]


def sample_mixer(h, c_cmp, c_sel, c_dkv, c_idx, s_win, page_table, w_in, cmp_pe, cmp_w1, cmp_b1, cmp_w2,
                 tab_nsa, tab_dsa, w_branch_nsa, w_branch_dsa, w_out):
    B, T, _ = h.shape
    past = page_table.shape[1] * c_cmp.shape[1]
    nq, ckv, skv, wkv, ng, dq, dkv, iq, ik, iw, mg = project(h, w_in)
    full = lambda pool, new: jnp.concatenate([gather_pages(pool, page_table), new.astype(pool.dtype)], axis=1)
    ccmp = compress_kv(full(c_cmp, ckv), cmp_pe, cmp_w1, cmp_b1, cmp_w2)
    sel = full(c_sel, skv)
    dkv_all = full(c_dkv, dkv)
    ik_all = full(c_idx, ik)
    w_buf = s_win.shape[1]
    win = jnp.concatenate([s_win, wkv.astype(s_win.dtype)], axis=1)
    qpos = past + jnp.arange(T)
    wpos = past - w_buf + jnp.arange(w_buf + T)
    k_top = min(DSA_TOPK_MAX, (past + T) // 4)
    o_n = nsa_attend(nq, qpos, ccmp[:, :, 0], ccmp[:, :, 1], sel[:, :, 0], sel[:, :, 1],
                     win[:, :, 0], win[:, :, 1], wpos, ng, tab_nsa)
    o_d = dsa_attend(dq, qpos, dkv_all[:, :, 0], dkv_all[:, :, 1], iq, ik_all, iw, k_top, tab_dsa)
    y = merge(o_n, o_d, mg, w_branch_nsa, w_branch_dsa, w_out)
    return y, ckv, skv, dkv, ik, win[:, T:]


def swiglu(h, w_gate, w_up, w_down):
    return (jax.nn.silu(h @ w_gate) * (h @ w_up)) @ w_down


def setup_inputs(seed: int = 0) -> dict:
    key = jax.random.key(seed)
    ks = jax.random.split(key, 24)
    f32 = jnp.float32
    nrm = lambda k, shape, sc: jax.random.normal(k, shape, f32) * sc
    n_pages = PAST_LEN // PAGE_SIZE
    n_used = DEC_BATCH * n_pages
    n_pool = n_used + n_used // 4
    w_buf = min(WINDOW, PAST_LEN)
    page_table = jax.random.permutation(ks[0], n_pool)[:n_used].reshape(DEC_BATCH, n_pages).astype(jnp.int32)
    return {
        'x_prompt': nrm(ks[1], (BATCH, SEQ, D_MODEL), 1.0),
        'x_sample': nrm(ks[2], (DEC_BATCH, DEC_SEQ, D_MODEL), 1.0),
        'cache_nsa_cmp': nrm(ks[3], (DEPTH, n_pool, PAGE_SIZE, 2, NSA_KV_HEADS, HEAD_DIM), 1.0),
        'cache_nsa_sel': nrm(ks[4], (DEPTH, n_pool, PAGE_SIZE, 2, NSA_KV_HEADS, HEAD_DIM), 1.0),
        'cache_dsa_kv': nrm(ks[5], (DEPTH, n_pool, PAGE_SIZE, 2, DSA_KV_HEADS, HEAD_DIM), 1.0),
        'cache_dsa_idx': nrm(ks[6], (DEPTH, n_pool, PAGE_SIZE, IDX_DIM), 1.0),
        'state_nsa_win': nrm(ks[7], (DEPTH, DEC_BATCH, w_buf, 2, NSA_KV_HEADS, HEAD_DIM), 1.0),
        'page_table': page_table,
        'norm_mix': 1.0 + nrm(ks[8], (DEPTH, D_MODEL), 0.02),
        'w_in': nrm(ks[9], (DEPTH, D_MODEL, D_IN), D_MODEL ** -0.5),
        'cmp_pe': nrm(ks[10], (DEPTH, 2, CMP_BLOCK, HEAD_DIM), 0.5),
        'cmp_w1': nrm(ks[11], (DEPTH, 2, CMP_BLOCK, HEAD_DIM, CMP_HIDDEN), (CMP_BLOCK * HEAD_DIM) ** -0.5),
        'cmp_b1': nrm(ks[12], (DEPTH, 2, CMP_HIDDEN), 0.02),
        'cmp_w2': nrm(ks[13], (DEPTH, 2, CMP_HIDDEN, HEAD_DIM), CMP_HIDDEN ** -0.5),
        'rel_bias': nrm(ks[14], (N_BUCKETS, NSA_HEADS + DSA_HEADS), 0.5),
        'w_branch_nsa': nrm(ks[15], (DEPTH, NSA_HEADS * HEAD_DIM, D_MODEL), (NSA_HEADS * HEAD_DIM) ** -0.5),
        'w_branch_dsa': nrm(ks[16], (DEPTH, DSA_HEADS * HEAD_DIM, D_MODEL), (DSA_HEADS * HEAD_DIM) ** -0.5),
        'w_out': nrm(ks[17], (DEPTH, D_MODEL, D_MODEL), D_MODEL ** -0.5),
        'norm_ffn': 1.0 + nrm(ks[18], (DEPTH, D_MODEL), 0.02),
        'w_gate': nrm(ks[19], (DEPTH, D_MODEL, D_FF), D_MODEL ** -0.5),
        'w_up': nrm(ks[20], (DEPTH, D_MODEL, D_FF), D_MODEL ** -0.5),
        'w_down': nrm(ks[21], (DEPTH, D_FF, D_MODEL), D_FF ** -0.5),
        'norm_final': 1.0 + nrm(ks[22], (D_MODEL,), 0.02),
    }


def reference(x_prompt, x_sample, cache_nsa_cmp, cache_nsa_sel, cache_dsa_kv, cache_dsa_idx, state_nsa_win,
              page_table, norm_mix, w_in, cmp_pe, cmp_w1, cmp_b1, cmp_w2, rel_bias, w_branch_nsa, w_branch_dsa,
              w_out, norm_ffn, w_gate, w_up, w_down, norm_final):
    tab_nsa = rel_bias[:, :NSA_HEADS].reshape(N_BUCKETS, NSA_KV_HEADS, NSA_GROUP)
    tab_dsa = rel_bias[:, NSA_HEADS:].reshape(N_BUCKETS, DSA_KV_HEADS, DSA_GROUP)
    xp, xs = x_prompt, x_sample
    new_p = [[] for _ in range(5)]
    new_s = [[] for _ in range(5)]
    for l in range(DEPTH):
        yp, *st_p = prompt_mixer(rmsnorm(xp, norm_mix[l]), w_in[l], cmp_pe[l], cmp_w1[l], cmp_b1[l], cmp_w2[l],
                                 tab_nsa, tab_dsa, w_branch_nsa[l], w_branch_dsa[l], w_out[l])
        ys, *st_s = sample_mixer(rmsnorm(xs, norm_mix[l]), cache_nsa_cmp[l], cache_nsa_sel[l], cache_dsa_kv[l],
                                 cache_dsa_idx[l], state_nsa_win[l], page_table, w_in[l], cmp_pe[l], cmp_w1[l],
                                 cmp_b1[l], cmp_w2[l], tab_nsa, tab_dsa, w_branch_nsa[l], w_branch_dsa[l], w_out[l])
        xp = xp + yp
        xs = xs + ys
        xp = xp + swiglu(rmsnorm(xp, norm_ffn[l]), w_gate[l], w_up[l], w_down[l])
        xs = xs + swiglu(rmsnorm(xs, norm_ffn[l]), w_gate[l], w_up[l], w_down[l])
        for i in range(5):
            new_p[i].append(st_p[i])
            new_s[i].append(st_s[i])
    y_prompt = rmsnorm(xp, norm_final)
    y_sample = rmsnorm(xs, norm_final)
    new_nsa_cmp_prompt = jnp.stack(new_p[0], axis=0)
    new_nsa_cmp_sample = jnp.stack(new_s[0], axis=0)
    new_nsa_sel_prompt = jnp.stack(new_p[1], axis=0)
    new_nsa_sel_sample = jnp.stack(new_s[1], axis=0)
    new_dsa_kv_prompt = jnp.stack(new_p[2], axis=0)
    new_dsa_kv_sample = jnp.stack(new_s[2], axis=0)
    new_dsa_idx_prompt = jnp.stack(new_p[3], axis=0)
    new_dsa_idx_sample = jnp.stack(new_s[3], axis=0)
    new_nsa_win_prompt = jnp.stack(new_p[4], axis=0)
    new_nsa_win_sample = jnp.stack(new_s[4], axis=0)
    return (y_prompt, y_sample, new_nsa_cmp_prompt, new_nsa_cmp_sample, new_nsa_sel_prompt, new_nsa_sel_sample,
            new_dsa_kv_prompt, new_dsa_kv_sample, new_dsa_idx_prompt, new_dsa_idx_sample,
            new_nsa_win_prompt, new_nsa_win_sample)
```

```python
import functools
import math

import jax
import jax.numpy as jnp
import numpy as np
from jax import lax
from jax.experimental import pallas as pl
from jax.experimental.pallas import tpu as pltpu

F32 = jnp.float32
BF16 = jnp.bfloat16

D_MODEL = 1024
HEAD_DIM = 64
NSA_HEADS = 8
NSA_KV_HEADS = 2
NSA_GROUP = NSA_HEADS // NSA_KV_HEADS
CMP_BLOCK = 32
CMP_STRIDE = 16
CMP_RATIO = CMP_BLOCK // CMP_STRIDE
CMP_HIDDEN = 2 * HEAD_DIM
SEL_BLOCK = 64
SEL_TOPK = 16
WINDOW = 512
DSA_HEADS = 8
DSA_KV_HEADS = 2
DSA_GROUP = DSA_HEADS // DSA_KV_HEADS
IDX_HEADS = 4
IDX_DIM = 64
DSA_TOPK_MAX = 256
N_BUCKETS = 32
MAX_DISTANCE = 128
Q_BLOCK = 128
EPS = 1e-6
SCALE = HEAD_DIM ** -0.5

KV_W = 2 * NSA_KV_HEADS * HEAD_DIM
Q_W = NSA_HEADS * HEAD_DIM
PROJ_WIDTHS = (Q_W, KV_W, KV_W, KV_W, 3 * NSA_HEADS, DSA_HEADS * HEAD_DIM, KV_W,
               IDX_HEADS * IDX_DIM, IDX_DIM, IDX_HEADS, 2 * D_MODEL)
PROJ_OFFS = tuple(int(v) for v in np.cumsum((0,) + PROJ_WIDTHS))

VMEM_LIMIT = 48 * 1024 * 1024
ROW_TILE = 512


def _rms(x, g):
    return x * lax.rsqrt(jnp.mean(x * x, axis=-1, keepdims=True) + EPS) * g


_MISC_W = 128
_PACK_W = 2 * Q_W + 4 * KV_W + IDX_HEADS * IDX_DIM + _MISC_W


def _pack_w_in(w_in):
    o = PROJ_OFFS
    seg = lambda i: w_in[:, o[i]:o[i + 1]]
    pad = jnp.zeros((D_MODEL, _MISC_W - IDX_DIM - IDX_HEADS - 3 * NSA_HEADS), w_in.dtype)
    cols = [seg(0), seg(1), seg(2), seg(3), seg(5), seg(6), seg(7), seg(8), seg(9), seg(4), pad]
    return jnp.concatenate(cols, axis=1).astype(BF16), seg(10).astype(BF16)


def _proj_body(x_ref, g_ref, w_ref, nq_ref, ckv_ref, skv_ref, wkv_ref, dq_ref, dkv_ref, iq_ref, ik_ref, misc_ref):
    h = _rms(x_ref[...], g_ref[...]).astype(BF16)
    mm = lambda lo, hi: jnp.dot(h, w_ref[:, lo:hi], preferred_element_type=F32)
    c = 0
    nq_ref[...] = (mm(c, c + Q_W) * SCALE).astype(BF16); c += Q_W
    ckv_ref[...] = mm(c, c + KV_W); c += KV_W
    skv_ref[...] = mm(c, c + KV_W); c += KV_W
    wkv_ref[...] = mm(c, c + KV_W); c += KV_W
    dq_ref[...] = (mm(c, c + Q_W) * SCALE).astype(BF16); c += Q_W
    dkv_ref[...] = mm(c, c + KV_W); c += KV_W
    iq_ref[...] = mm(c, c + IDX_HEADS * IDX_DIM).astype(BF16); c += IDX_HEADS * IDX_DIM
    m = mm(c, c + _MISC_W)
    ik_ref[...] = m[:, :IDX_DIM]
    misc_ref[...] = m


def _project(x2d, g, w_pack):
    rows = x2d.shape[0]
    tm = min(ROW_TILE, rows)
    row = lambda w: pl.BlockSpec((tm, w), lambda i: (i, 0))
    full = lambda a: pl.BlockSpec(a.shape, lambda i: (0, 0))
    widths = (Q_W, KV_W, KV_W, KV_W, Q_W, KV_W, IDX_HEADS * IDX_DIM, IDX_DIM, _MISC_W)
    dtypes = (BF16, F32, F32, F32, BF16, F32, BF16, F32, F32)
    return pl.pallas_call(
        _proj_body,
        grid=(rows // tm,),
        in_specs=[row(D_MODEL), full(g), full(w_pack)],
        out_specs=[row(w) for w in widths],
        out_shape=[jax.ShapeDtypeStruct((rows, w), d) for w, d in zip(widths, dtypes)],
        compiler_params=pltpu.CompilerParams(dimension_semantics=("parallel",), vmem_limit_bytes=VMEM_LIMIT),
        name="proj",
    )(x2d, g, w_pack)


def _merge_body(x_ref, on_ref, od_ref, g_ref, wmg_ref, wa_ref, wb_ref, wo_ref, out_ref):
    x = x_ref[...]
    h = _rms(x, g_ref[...]).astype(BF16)
    dot = lambda a, b: jnp.dot(a, b, preferred_element_type=F32)
    m = jax.nn.sigmoid(dot(h, wmg_ref[:, :D_MODEL])) * dot(on_ref[...], wa_ref[...])
    m = m + jax.nn.sigmoid(dot(h, wmg_ref[:, D_MODEL:])) * dot(od_ref[...], wb_ref[...])
    out_ref[...] = x + dot(m.astype(BF16), wo_ref[...])


def _merge(x2d, o_nsa, o_dsa, g, w_mg, w_a, w_b, w_o):
    rows = x2d.shape[0]
    tm = min(ROW_TILE, rows)
    row = lambda w: pl.BlockSpec((tm, w), lambda i: (i, 0))
    full = lambda a: pl.BlockSpec(a.shape, lambda i: (0, 0))
    return pl.pallas_call(
        _merge_body,
        grid=(rows // tm,),
        in_specs=[row(D_MODEL), row(Q_W), row(Q_W), full(g), full(w_mg), full(w_a), full(w_b), full(w_o)],
        out_specs=row(D_MODEL),
        out_shape=jax.ShapeDtypeStruct((rows, D_MODEL), F32),
        compiler_params=pltpu.CompilerParams(dimension_semantics=("parallel",), vmem_limit_bytes=VMEM_LIMIT),
        name="merge",
    )(x2d, o_nsa, o_dsa, g, w_mg, w_a, w_b, w_o)


def _ffn_body(n_chunks, x_ref, g_ref, gf_ref, wg_ref, wu_ref, wd_ref, out_ref):
    x = x_ref[...]
    h = _rms(x, g_ref[...]).astype(BF16)
    dot = lambda a, b: jnp.dot(a, b, preferred_element_type=F32)
    fc = wg_ref.shape[1] // n_chunks
    acc = x
    for c in range(n_chunks):
        gate = dot(h, wg_ref[:, c * fc:(c + 1) * fc])
        up = dot(h, wu_ref[:, c * fc:(c + 1) * fc])
        act = (gate * jax.nn.sigmoid(gate) * up).astype(BF16)
        acc = acc + dot(act, wd_ref[c * fc:(c + 1) * fc, :])
    out_ref[...] = _rms(acc, gf_ref[...])


def _ffn(x2d, g, g_final, w_gate, w_up, w_down):
    rows = x2d.shape[0]
    tm = min(ROW_TILE, rows)
    d_ff = w_gate.shape[1]
    n_chunks = 2 if d_ff % 256 == 0 else 1
    row = lambda w: pl.BlockSpec((tm, w), lambda i: (i, 0))
    full = lambda a: pl.BlockSpec(a.shape, lambda i: (0, 0))
    return pl.pallas_call(
        functools.partial(_ffn_body, n_chunks),
        grid=(rows // tm,),
        in_specs=[row(D_MODEL), full(g), full(g_final), full(w_gate), full(w_up), full(w_down)],
        out_specs=row(D_MODEL),
        out_shape=jax.ShapeDtypeStruct((rows, D_MODEL), F32),
        compiler_params=pltpu.CompilerParams(dimension_semantics=("parallel",), vmem_limit_bytes=56 * 1024 * 1024),
        name="ffn",
    )(x2d, g, g_final, w_gate, w_up, w_down)


def _t5_bucket(dist):
    d = jnp.maximum(dist, 0)
    max_exact = N_BUCKETS // 2
    logd = jnp.log(jnp.maximum(d, 1).astype(F32) / max_exact) / math.log(MAX_DISTANCE / max_exact)
    large = jnp.minimum(max_exact + (logd * (N_BUCKETS - max_exact)).astype(jnp.int32), N_BUCKETS - 1)
    return jnp.where(d < max_exact, d, large)


def _masked_softmax(s, mask):
    s = jnp.where(mask, s.astype(F32), -jnp.inf)
    m = jnp.max(s, axis=-1, keepdims=True)
    p = jnp.exp(s - jnp.where(jnp.isfinite(m), m, 0.0))
    return p / jnp.maximum(jnp.sum(p, axis=-1, keepdims=True), 1e-30)


def _gather_pages(pool, page_table):
    g = pool[page_table]
    return g.reshape((g.shape[0], g.shape[1] * g.shape[2]) + g.shape[3:])


def _compress_kv(kv, cmp_pe, cmp_w1, cmp_b1, cmp_w2):
    B, L = kv.shape[:2]
    n_chunks = L // CMP_STRIDE
    n_cmp = n_chunks - CMP_RATIO + 1
    chunks = kv[:, :n_chunks * CMP_STRIDE].reshape(B, n_chunks, CMP_STRIDE, 2, kv.shape[3], HEAD_DIM)
    w1 = cmp_w1.reshape(2, CMP_RATIO, CMP_STRIDE, HEAD_DIM, CMP_HIDDEN)
    h = (cmp_b1 + jnp.einsum('cld,cldf->cf', cmp_pe, cmp_w1))[None, None, :, None, :]
    for r in range(CMP_RATIO):
        h = h + jnp.einsum('bnlchd,cldf->bnchf', chunks, w1[:, r])[:, r:r + n_cmp]
    return jnp.einsum('bnchf,cfd->bnchd', jax.nn.gelu(h), cmp_w2)


def _overlap_matrix(n_cmp, n_sel):
    c0 = np.arange(n_cmp)[:, None] * CMP_STRIDE
    s0 = np.arange(n_sel)[None, :] * SEL_BLOCK
    m = (c0 < s0 + SEL_BLOCK) & (c0 + CMP_BLOCK > s0)
    return jnp.asarray(m.astype(np.float32))


def _nsa_attend(q, qpos, ck, cv, sk, sv, wk, wv, wpos, gates, tab):
    B, Tq, n_kv = q.shape[:3]
    n_cmp, L = ck.shape[1], sk.shape[1]
    n_sel = -(-L // SEL_BLOCK)
    k_sel = min(SEL_TOPK, n_sel)
    scale = 1.0
    cend = jnp.arange(n_cmp) * CMP_STRIDE + (CMP_BLOCK - 1)
    dist = qpos[:, None] - cend[None, :]
    bias = jnp.transpose(tab[_t5_bucket(dist)], (2, 3, 0, 1))
    s = jnp.einsum('bthgd,bnhd->bhgtn', q, ck) * scale + bias
    p_cmp = _masked_softmax(s, dist >= 0)
    o_cmp = jnp.einsum('bhgtn,bnhd->bthgd', p_cmp.astype(cv.dtype), cv)
    imp = jnp.einsum('bhgtn,nj->bhtj', p_cmp, _overlap_matrix(n_cmp, n_sel))
    blk = jnp.arange(n_sel)[None, :]
    cur = (qpos // SEL_BLOCK)[:, None]
    forced = (blk == 0) | (blk == cur) | (blk == cur - 1)
    imp = jnp.where(forced, jnp.inf, jnp.where(blk <= cur, imp, -jnp.inf))
    _, sel = lax.top_k(imp, k_sel)
    tok = (sel[..., None] * SEL_BLOCK + jnp.arange(SEL_BLOCK)).reshape(B, n_kv, Tq, k_sel * SEL_BLOCK)
    tokc = jnp.minimum(tok, L - 1)
    bi = jnp.arange(B)[:, None, None, None]
    hi = jnp.arange(n_kv)[None, :, None, None]
    ks = sk[bi, tokc, hi]
    vs = sv[bi, tokc, hi]
    dist = qpos[None, None, :, None] - tok
    bias = jnp.transpose(tab[_t5_bucket(dist), hi], (0, 1, 4, 2, 3))
    s = jnp.einsum('bthgd,bhtsd->bhgts', q, ks) * scale + bias
    p = _masked_softmax(s, (dist >= 0)[:, :, None])
    o_sel = jnp.einsum('bhgts,bhtsd->bthgd', p.astype(vs.dtype), vs)
    dist = qpos[:, None] - wpos[None, :]
    mask = (dist >= 0) & (dist < WINDOW) & (wpos >= 0)[None, :]
    bias = jnp.transpose(tab[_t5_bucket(dist)], (2, 3, 0, 1))
    s = jnp.einsum('bthgd,bshd->bhgts', q, wk) * scale + bias
    p = _masked_softmax(s, mask)
    o_win = jnp.einsum('bhgts,bshd->bthgd', p.astype(wv.dtype), wv)
    o = gates[..., 0:1] * o_cmp + gates[..., 1:2] * o_sel + gates[..., 2:3] * o_win
    return o.reshape(B, Tq, -1)


def _dsa_attend(q, qpos, k, v, iq, ik, iw, k_top, tab):
    B, Tq = q.shape[:2]
    L = k.shape[1]
    causal = jnp.arange(L)[None, :] <= qpos[:, None]
    score = jnp.einsum('btj,btjs->bts', iw, jax.nn.relu(jnp.einsum('btji,bsi->btjs', iq, ik)))
    score = jnp.where(causal, score.astype(F32), -jnp.inf)
    _, top = lax.top_k(score, k_top)
    bi = jnp.arange(B)[:, None, None]
    ks = k[bi, top]
    vs = v[bi, top]
    dist = qpos[None, :, None] - top
    bias = jnp.transpose(tab[_t5_bucket(dist)], (0, 3, 4, 1, 2))
    s = jnp.einsum('bthgd,btshd->bhgts', q, ks) + bias
    p = _masked_softmax(s, (dist >= 0)[:, None, None])
    o = jnp.einsum('bhgts,btshd->bthgd', p.astype(vs.dtype), vs)
    return o.reshape(B, Tq, -1)


def _split_proj(outs, B, T):
    nq, ckv, skv, wkv, dq, dkv, iq, ik, misc = outs
    kv5 = lambda a: a.reshape(B, T, 2, NSA_KV_HEADS, HEAD_DIM)
    ng = jax.nn.sigmoid(misc[:, IDX_DIM + IDX_HEADS:IDX_DIM + IDX_HEADS + 3 * NSA_HEADS])
    iw = misc[:, IDX_DIM:IDX_DIM + IDX_HEADS] * IDX_HEADS ** -0.5
    return (nq.astype(F32).reshape(B, T, NSA_KV_HEADS, NSA_GROUP, HEAD_DIM), kv5(ckv), kv5(skv), kv5(wkv),
            ng.reshape(B, T, NSA_KV_HEADS, NSA_GROUP, 3),
            dq.astype(F32).reshape(B, T, DSA_KV_HEADS, DSA_GROUP, HEAD_DIM), kv5(dkv),
            iq.astype(F32).reshape(B, T, IDX_HEADS, IDX_DIM), ik.reshape(B, T, IDX_DIM),
            iw.reshape(B, T, IDX_HEADS))


def _prompt_attention(p, cmp_pe, cmp_w1, cmp_b1, cmp_w2, tab_nsa, tab_dsa):
    nq, ckv, skv, wkv, ng, dq, dkv, iq, ik, iw = p
    B, T = nq.shape[:2]
    ccmp = _compress_kv(ckv, cmp_pe, cmp_w1, cmp_b1, cmp_w2)
    ck, cv = ccmp[:, :, 0], ccmp[:, :, 1]
    sk, sv = skv[:, :, 0], skv[:, :, 1]
    dk, dv = dkv[:, :, 0], dkv[:, :, 1]
    win_pad = jnp.pad(wkv, ((0, 0), (WINDOW, 0), (0, 0), (0, 0), (0, 0)))
    k_top = min(DSA_TOPK_MAX, T // 4)

    def block(qb):
        t0 = qb * Q_BLOCK
        sl = lambda a: lax.dynamic_slice_in_dim(a, t0, Q_BLOCK, axis=1)
        qpos = t0 + jnp.arange(Q_BLOCK)
        w = lax.dynamic_slice_in_dim(win_pad, t0, WINDOW + Q_BLOCK, axis=1)
        wpos = t0 - WINDOW + jnp.arange(WINDOW + Q_BLOCK)
        o_n = _nsa_attend(sl(nq), qpos, ck, cv, sk, sv, w[:, :, 0], w[:, :, 1], wpos, sl(ng), tab_nsa)
        o_d = _dsa_attend(sl(dq), qpos, dk, dv, sl(iq), ik, sl(iw), k_top, tab_dsa)
        return o_n, o_d

    o_n, o_d = lax.map(block, jnp.arange(T // Q_BLOCK))
    fix = lambda y: jnp.moveaxis(y, 0, 1).reshape(B * T, -1).astype(BF16)
    return fix(o_n), fix(o_d)


def _sample_attention(p, c_cmp, c_sel, c_dkv, c_idx, s_win, page_table, cmp_pe, cmp_w1, cmp_b1, cmp_w2,
                      tab_nsa, tab_dsa):
    nq, ckv, skv, wkv, ng, dq, dkv, iq, ik, iw = p
    B, T = nq.shape[:2]
    past = page_table.shape[1] * c_cmp.shape[1]
    full = lambda pool, new: jnp.concatenate([_gather_pages(pool, page_table), new.astype(pool.dtype)], axis=1)
    ccmp = _compress_kv(full(c_cmp, ckv), cmp_pe, cmp_w1, cmp_b1, cmp_w2)
    sel = full(c_sel, skv)
    dkv_all = full(c_dkv, dkv)
    ik_all = full(c_idx, ik)
    w_buf = s_win.shape[1]
    win = jnp.concatenate([s_win, wkv.astype(s_win.dtype)], axis=1)
    qpos = past + jnp.arange(T)
    wpos = past - w_buf + jnp.arange(w_buf + T)
    k_top = min(DSA_TOPK_MAX, (past + T) // 4)
    o_n = _nsa_attend(nq, qpos, ccmp[:, :, 0], ccmp[:, :, 1], sel[:, :, 0], sel[:, :, 1],
                      win[:, :, 0], win[:, :, 1], wpos, ng, tab_nsa)
    o_d = _dsa_attend(dq, qpos, dkv_all[:, :, 0], dkv_all[:, :, 1], iq, ik_all, iw, k_top, tab_dsa)
    return o_n.reshape(B * T, -1).astype(BF16), o_d.reshape(B * T, -1).astype(BF16), win[:, T:]


def kernel(x_prompt, x_sample, cache_nsa_cmp, cache_nsa_sel, cache_dsa_kv, cache_dsa_idx, state_nsa_win, page_table, norm_mix, w_in, cmp_pe, cmp_w1, cmp_b1, cmp_w2, rel_bias, w_branch_nsa, w_branch_dsa, w_out, norm_ffn, w_gate, w_up, w_down, norm_final):
    assert norm_mix.shape[0] == 1, "single trunk layer"
    B, T, _ = x_prompt.shape
    DB, DT, _ = x_sample.shape
    tab_nsa = rel_bias[:, :NSA_HEADS].reshape(N_BUCKETS, NSA_KV_HEADS, NSA_GROUP)
    tab_dsa = rel_bias[:, NSA_HEADS:].reshape(N_BUCKETS, DSA_KV_HEADS, DSA_GROUP)
    l = 0
    g_mix = norm_mix[l][None]
    g_ffn = norm_ffn[l][None]
    g_fin = norm_final[None]
    w_pack, w_mg = _pack_w_in(w_in[l])
    w_a, w_b, w_o = (w.astype(BF16) for w in (w_branch_nsa[l], w_branch_dsa[l], w_out[l]))
    wg, wu, wd = (w.astype(BF16) for w in (w_gate[l], w_up[l], w_down[l]))
    cw = (cmp_pe[l], cmp_w1[l], cmp_b1[l], cmp_w2[l])

    def tail(x2d, o_n, o_d):
        x1 = _merge(x2d, o_n, o_d, g_mix, w_mg, w_a, w_b, w_o)
        return _ffn(x1, g_ffn, g_fin, wg, wu, wd)

    xp = x_prompt.reshape(B * T, D_MODEL)
    outs_p = _project(xp, g_mix, w_pack)
    pp = _split_proj(outs_p, B, T)
    o_n, o_d = _prompt_attention(pp, *cw, tab_nsa, tab_dsa)
    y_prompt = tail(xp, o_n, o_d).reshape(B, T, D_MODEL)

    xs = x_sample.reshape(DB * DT, D_MODEL)
    outs_s = _project(xs, g_mix, w_pack)
    ps = _split_proj(outs_s, DB, DT)
    o_n, o_d, win_s = _sample_attention(ps, cache_nsa_cmp[l], cache_nsa_sel[l], cache_dsa_kv[l], cache_dsa_idx[l],
                                        state_nsa_win[l], page_table, *cw, tab_nsa, tab_dsa)
    y_sample = tail(xs, o_n, o_d).reshape(DB, DT, D_MODEL)

    kv6 = lambda a, b, t: a.reshape(1, b, t, 2, NSA_KV_HEADS, HEAD_DIM)
    w_keep = min(WINDOW, T)
    return (y_prompt, y_sample,
            kv6(outs_p[1], B, T), kv6(outs_s[1], DB, DT),
            kv6(outs_p[2], B, T), kv6(outs_s[2], DB, DT),
            kv6(outs_p[5], B, T), kv6(outs_s[5], DB, DT),
            outs_p[7].reshape(1, B, T, IDX_DIM), outs_s[7].reshape(1, DB, DT, IDX_DIM),
            kv6(outs_p[3], B, T)[:, :, T - w_keep:], win_s[None])
```

```python
import functools
import math

import jax
import jax.numpy as jnp
import numpy as np
from jax import lax
from jax.experimental import pallas as pl
from jax.experimental.pallas import tpu as pltpu

F32 = jnp.float32
BF16 = jnp.bfloat16
I32 = jnp.int32

D_MODEL = 1024
HEAD_DIM = 64
NSA_HEADS = 8
NSA_KV_HEADS = 2
NSA_GROUP = NSA_HEADS // NSA_KV_HEADS
CMP_BLOCK = 32
CMP_STRIDE = 16
CMP_RATIO = CMP_BLOCK // CMP_STRIDE
CMP_HIDDEN = 2 * HEAD_DIM
SEL_BLOCK = 64
SEL_TOPK = 16
WINDOW = 512
DSA_HEADS = 8
DSA_KV_HEADS = 2
DSA_GROUP = DSA_HEADS // DSA_KV_HEADS
IDX_HEADS = 4
IDX_DIM = 64
DSA_TOPK_MAX = 256
N_BUCKETS = 32
MAX_DISTANCE = 128
Q_BLOCK = 128
EPS = 1e-6
SCALE = HEAD_DIM ** -0.5

LANES = 128
KV_W = 2 * NSA_KV_HEADS * HEAD_DIM
HALF = KV_W // 2
N_HEADS = NSA_HEADS
GROUP = NSA_GROUP
Q_EXT = N_HEADS * LANES
KEY_TILE = 512
KEY_PAD = KEY_TILE - Q_BLOCK
CMP_PAD = 128
CMP_NEAR = 256
PROJ_WIDTHS = (NSA_HEADS * HEAD_DIM, KV_W, KV_W, KV_W, 3 * NSA_HEADS, DSA_HEADS * HEAD_DIM, KV_W,
               IDX_HEADS * IDX_DIM, IDX_DIM, IDX_HEADS, 2 * D_MODEL)
PROJ_OFFS = tuple(int(v) for v in np.cumsum((0,) + PROJ_WIDTHS))
INT_MIN = -2 ** 31
NEG_INF = float("-inf")

VMEM_LIMIT = 56 * 1024 * 1024
ROW_TILE = 512


def _rms(x, g):
    return x * lax.rsqrt(jnp.mean(x * x, axis=-1, keepdims=True) + EPS) * g


def _dot(a, b):
    return jnp.dot(a, b, preferred_element_type=F32)


def _dot_nt(a, b):
    return lax.dot_general(a, b, (((1,), (1,)), ((), ())), preferred_element_type=F32)


def _full_spec(a):
    nd = a.ndim
    return pl.BlockSpec(a.shape, lambda *_: (0,) * nd)


_MISC_W = LANES
_IW_COL = 0
_NG_COL = IDX_HEADS
_PROJ_OUT = (
    ("nq", Q_EXT, BF16), ("dq", Q_EXT, BF16), ("iq", IDX_HEADS * LANES, BF16), ("ikx", LANES, BF16),
    ("ckv", KV_W, F32), ("skv", KV_W, F32), ("wkv", KV_W, F32), ("dkv", KV_W, F32), ("misc", _MISC_W, F32))


def _ext_cols(w, n_heads, kv_of_head):
    d = w.shape[0]
    w4 = w.reshape(d, n_heads, 1, HEAD_DIM)
    sel = jnp.asarray(np.eye(2, dtype=np.float32)[[kv_of_head(a) for a in range(n_heads)]], w.dtype)
    return (w4 * sel[None, :, :, None]).reshape(d, n_heads * LANES)


def _pack_w_in(w_in):
    o = PROJ_OFFS
    seg = lambda i: w_in[:, o[i]:o[i + 1]]
    zeros = lambda n: jnp.zeros((D_MODEL, n), w_in.dtype)
    cols = [_ext_cols(seg(0), N_HEADS, lambda a: a // GROUP),
            _ext_cols(seg(5), N_HEADS, lambda a: a // GROUP),
            _ext_cols(seg(7), IDX_HEADS, lambda a: 0),
            seg(8), zeros(LANES - IDX_DIM),
            seg(1), seg(2), seg(3), seg(6),
            seg(9), seg(4), zeros(_MISC_W - IDX_HEADS - 3 * NSA_HEADS)]
    return jnp.concatenate(cols, axis=1).astype(BF16), seg(10).astype(BF16)


def _proj_body(x_ref, g_ref, w_ref, nq_ref, dq_ref, iq_ref, ikx_ref, ckv_ref, skv_ref, wkv_ref, dkv_ref, misc_ref,
               ik_ref, skvb_ref, wkvb_ref, dkvb_ref):
    h = _rms(x_ref[...], g_ref[...]).astype(BF16)
    c = 0

    def mm(w):
        nonlocal c
        r = _dot(h, w_ref[:, c:c + w])
        c += w
        return r

    nq_ref[...] = (mm(Q_EXT) * SCALE).astype(BF16)
    dq_ref[...] = (mm(Q_EXT) * SCALE).astype(BF16)
    iq_ref[...] = mm(IDX_HEADS * LANES).astype(BF16)
    ik = mm(LANES)
    ikx_ref[...] = ik.astype(BF16)
    ik_ref[...] = ik[:, :IDX_DIM]
    ckv_ref[...] = mm(KV_W)
    skv = mm(KV_W)
    skv_ref[...] = skv
    skvb_ref[...] = skv.astype(BF16)
    wkv = mm(KV_W)
    wkv_ref[...] = wkv
    wkvb_ref[...] = wkv.astype(BF16)
    dkv = mm(KV_W)
    dkv_ref[...] = dkv
    dkvb_ref[...] = dkv.astype(BF16)
    misc_ref[...] = mm(_MISC_W)


def _project(x2d, g, w_pack):
    rows = x2d.shape[0]
    tm = min(ROW_TILE, rows)
    row = lambda w: pl.BlockSpec((tm, w), lambda i: (i, 0))
    outs = list(_PROJ_OUT) + [("ik", IDX_DIM, F32), ("skvb", KV_W, BF16), ("wkvb", KV_W, BF16), ("dkvb", KV_W, BF16)]
    res = pl.pallas_call(
        _proj_body,
        grid=(rows // tm,),
        in_specs=[row(D_MODEL), _full_spec(g), _full_spec(w_pack)],
        out_specs=[row(w) for _, w, _ in outs],
        out_shape=[jax.ShapeDtypeStruct((rows, w), d) for _, w, d in outs],
        compiler_params=pltpu.CompilerParams(dimension_semantics=("parallel",), vmem_limit_bytes=VMEM_LIMIT),
        name="proj",
    )(x2d, g, w_pack)
    return {name: r for (name, _, _), r in zip(outs, res)}


def _np_bucket(dist):
    d = np.maximum(np.asarray(dist, np.int64), 0)
    max_exact = N_BUCKETS // 2
    logd = np.log(np.maximum(d, 1).astype(np.float64) / max_exact) / math.log(MAX_DISTANCE / max_exact)
    large = np.minimum(max_exact + (logd * (N_BUCKETS - max_exact)).astype(np.int64), N_BUCKETS - 1)
    return np.where(d < max_exact, d, large).astype(np.int32)


_FAR_DIST = 129
assert int(_np_bucket(_FAR_DIST - 16).min()) == N_BUCKETS - 1


def _head_major(t):
    return jnp.moveaxis(t, -1, 0)


def _prompt_tables(tab):
    i = np.arange(Q_BLOCK)[:, None]
    j = np.arange(Q_BLOCK)[None, :]
    m0 = _head_major(tab[_np_bucket(i - j)])
    m1 = _head_major(tab[_np_bucket(Q_BLOCK + i - j)])
    c = np.arange(CMP_NEAR)[None, :]
    tc = _head_major(tab[_np_bucket(i - CMP_STRIDE * (c - CMP_PAD) - (CMP_BLOCK - 1))])
    far = jnp.broadcast_to(tab[N_BUCKETS - 1][:, None], (tab.shape[1], LANES))
    return m0, m1, tc, far


def _pack_cmp_weights(cmp_pe, cmp_w1, cmp_b1, cmp_w2):
    eye = jnp.eye(NSA_KV_HEADS, dtype=cmp_w1.dtype)
    w1r = cmp_w1.reshape(2, CMP_RATIO, CMP_STRIDE, HEAD_DIM, CMP_HIDDEN)
    w1 = jnp.einsum('crldf,hk->clhdrkf', w1r, eye).reshape(
        2, CMP_STRIDE * HALF, CMP_RATIO * NSA_KV_HEADS * CMP_HIDDEN)
    w2 = jnp.einsum('cfd,hk->chfkd', cmp_w2, eye).reshape(2, NSA_KV_HEADS * CMP_HIDDEN, HALF)
    pe = cmp_pe.reshape(2, CMP_RATIO, CMP_STRIDE, HEAD_DIM)
    pex = jnp.broadcast_to(jnp.transpose(pe, (1, 2, 0, 3))[:, :, :, None, :],
                           (CMP_RATIO, CMP_STRIDE, 2, NSA_KV_HEADS, HEAD_DIM)).reshape(CMP_RATIO, CMP_STRIDE * KV_W)
    pex = jnp.concatenate([pex, jnp.zeros((8 - CMP_RATIO, CMP_STRIDE * KV_W), pex.dtype)], axis=0)
    b1 = jnp.concatenate([cmp_b1] * NSA_KV_HEADS, axis=-1)[:, None, :]
    return w1.astype(BF16), w2.astype(BF16), pex, b1


def _compress_chunks(x, pex, w1_ref, w2_ref, b1_ref):
    nc = x.shape[0]
    hw = NSA_KV_HEADS * CMP_HIDDEN
    outs = []
    for c in range(2):
        cols = lambda a: jnp.concatenate(
            [a[:, l * KV_W + c * HALF:l * KV_W + (c + 1) * HALF] for l in range(CMP_STRIDE)], axis=1).astype(BF16)
        hr = _dot(cols(x), w1_ref[c])
        hp = _dot(cols(pex), w1_ref[c])
        bias = b1_ref[c] + hp[0:1, 0:hw] + hp[1:2, hw:2 * hw]
        hsum = hr[:, 0:hw] + pltpu.roll(hr[:, hw:2 * hw], nc - 1, axis=0) + bias
        outs.append(_dot(jax.nn.gelu(hsum, approximate=True).astype(BF16), w2_ref[c]))
    return outs


def _compress_body(x_ref, pex_ref, w1_ref, w2_ref, b1_ref, out_ref):
    nc = x_ref.shape[1]
    o0, o1 = _compress_chunks(x_ref[0], pex_ref[...], w1_ref, w2_ref, b1_ref)
    out_ref[0, 0:CMP_PAD, :] = jnp.zeros((CMP_PAD, KV_W), BF16)
    out_ref[0, CMP_PAD + nc:, :] = jnp.zeros((CMP_PAD, KV_W), BF16)
    out_ref[0, CMP_PAD:CMP_PAD + nc, 0:HALF] = o0.astype(BF16)
    out_ref[0, CMP_PAD:CMP_PAD + nc, HALF:KV_W] = o1.astype(BF16)


def _compress_prompt(ckv3, cw):
    w1, w2, pex, b1 = cw
    B, T, _ = ckv3.shape
    nc = T // CMP_STRIDE
    x = ckv3.reshape(B, nc, CMP_STRIDE * KV_W)
    return pl.pallas_call(
        _compress_body,
        grid=(B,),
        in_specs=[pl.BlockSpec((1, nc, CMP_STRIDE * KV_W), lambda b: (b, 0, 0)),
                  _full_spec(pex), _full_spec(w1), _full_spec(w2), _full_spec(b1)],
        out_specs=pl.BlockSpec((1, nc + 2 * CMP_PAD, KV_W), lambda b: (b, 0, 0)),
        out_shape=jax.ShapeDtypeStruct((B, nc + 2 * CMP_PAD, KV_W), BF16),
        compiler_params=pltpu.CompilerParams(dimension_semantics=("parallel",), vmem_limit_bytes=VMEM_LIMIT),
        name="compress_prompt",
    )(x, pex, w1, w2, b1)


def _stack_heads(q_ref, kv):
    return jnp.concatenate([q_ref[:, (kv * GROUP + g) * LANES:(kv * GROUP + g + 1) * LANES] for g in range(GROUP)],
                           axis=0)


def _softmax_rows(s):
    m = jnp.max(s, axis=-1, keepdims=True)
    p = jnp.exp(s - jnp.where(m == NEG_INF, 0.0, m))
    return p / jnp.maximum(jnp.sum(p, axis=-1, keepdims=True), 1e-30)


def _softmax_pair(a, b):
    m = jnp.maximum(jnp.max(a, axis=-1, keepdims=True), jnp.max(b, axis=-1, keepdims=True))
    m = jnp.where(m == NEG_INF, 0.0, m)
    pa = jnp.exp(a - m)
    pb = jnp.exp(b - m)
    l = jnp.maximum(jnp.sum(pa, axis=-1, keepdims=True) + jnp.sum(pb, axis=-1, keepdims=True), 1e-30)
    return pa / l, pb / l


def _select_topk(val, k):
    col = lax.broadcasted_iota(I32, val.shape, 1).astype(F32)
    sel = jnp.zeros(val.shape, F32)
    for _ in range(k):
        mx = jnp.max(val, axis=-1, keepdims=True)
        first = jnp.min(jnp.where(val == mx, col, float(val.shape[1])), axis=-1, keepdims=True)
        hit = col == first
        sel = jnp.where(hit, 1.0, sel)
        val = jnp.where(hit, NEG_INF, val)
    return sel


def _online_update(carry, s, v2):
    m, l, acc = carry
    m_new = jnp.maximum(m, jnp.max(s, axis=-1, keepdims=True))
    m_safe = jnp.where(m_new == NEG_INF, 0.0, m_new)
    alpha = jnp.exp(m - m_safe)
    p = jnp.exp(s - m_safe)
    l = alpha * l + jnp.sum(p, axis=-1, keepdims=True)
    acc = alpha * acc + _dot(p.astype(BF16), v2)
    return m_new, l, acc


def _online_init(rows):
    return (jnp.full((rows, 1), NEG_INF, F32), jnp.zeros((rows, 1), F32), jnp.zeros((rows, LANES), F32))


def _online_finish(carry):
    _, l, acc = carry
    return acc / jnp.maximum(l, 1e-30)


def _near_bias(kv, m0_ref, m1_ref, far_ref, n_far_cols):
    rows = []
    for g in range(GROUP):
        a = kv * GROUP + g
        parts = [m1_ref[a], m0_ref[a]]
        if n_far_cols:
            parts = [jnp.broadcast_to(far_ref[a:a + 1, :], (Q_BLOCK, LANES))] * (n_far_cols // LANES) + parts
        rows.append(jnp.concatenate(parts, axis=1))
    return jnp.concatenate(rows, axis=0)


def _far_col(kv, far_ref):
    return jnp.concatenate(
        [jnp.broadcast_to(far_ref[kv * GROUP + g:kv * GROUP + g + 1, 0:1], (Q_BLOCK, 1)) for g in range(GROUP)], axis=0)


def _mask_rows(keep, s):
    n = s.shape[1]
    return jnp.where(keep[None], s.reshape(GROUP, Q_BLOCK, n), NEG_INF).reshape(GROUP * Q_BLOCK, n)


def _nsa_body(q_ref, misc_ref, ccmp_ref, skv_ref, wkv_ref, ovl_ref, tc_ref, m0_ref, m1_ref, far_ref, o_ref):
    qb = pl.program_id(1)
    t0 = qb * Q_BLOCK
    cw = ccmp_ref.shape[1]
    n_tiles = qb // (KEY_TILE // Q_BLOCK) + 1
    gates = jax.nn.sigmoid(misc_ref[:, _NG_COL:_NG_COL + 3 * NSA_HEADS])

    row_c = lax.broadcasted_iota(I32, (Q_BLOCK, cw), 0)
    col_c = lax.broadcasted_iota(I32, (Q_BLOCK, cw), 1)
    n_c = col_c - CMP_PAD
    cmp_valid = (n_c >= 0) & (CMP_STRIDE * n_c + (CMP_BLOCK - 1) <= t0 + row_c)

    row_b = lax.broadcasted_iota(I32, (Q_BLOCK, LANES), 0)
    blk = lax.broadcasted_iota(I32, (Q_BLOCK, LANES), 1)
    cur = (t0 + row_b) // SEL_BLOCK
    forced = (blk == 0) | (blk == cur) | (blk == cur - 1)

    row_t = lax.broadcasted_iota(I32, (Q_BLOCK, KEY_TILE), 0)
    col_t = lax.broadcasted_iota(I32, (Q_BLOCK, KEY_TILE), 1)
    e_row = lax.broadcasted_iota(I32, (LANES, KEY_TILE), 0)
    e_col = lax.broadcasted_iota(I32, (LANES, KEY_TILE), 1) // SEL_BLOCK

    row_w = lax.broadcasted_iota(I32, (Q_BLOCK, WINDOW + Q_BLOCK), 0)
    col_w = lax.broadcasted_iota(I32, (Q_BLOCK, WINDOW + Q_BLOCK), 1)
    win_valid = (col_w <= WINDOW + row_w) & (col_w > row_w) & (col_w >= WINDOW - t0)

    for kv in range(NSA_KV_HEADS):
        q = _stack_heads(q_ref, kv)
        lane_keep = (lax.broadcasted_iota(I32, (Q_BLOCK, LANES), 1) // HEAD_DIM == kv).astype(F32)

        s = _dot_nt(q, ccmp_ref[0, :, 0:HALF])
        cv2 = ccmp_ref[0, :, HALF:KV_W]
        o_cmp = []
        imp = jnp.zeros((Q_BLOCK, LANES), F32)
        for g in range(GROUP):
            a = kv * GROUP + g
            table = jnp.concatenate(
                [tc_ref[a]] + [jnp.broadcast_to(far_ref[a:a + 1, :], (Q_BLOCK, LANES))] * ((cw - CMP_NEAR) // LANES),
                axis=1)
            bias = pltpu.roll(table, (Q_BLOCK // CMP_STRIDE) * qb, axis=1)
            sg = jnp.where(cmp_valid, s[g * Q_BLOCK:(g + 1) * Q_BLOCK] + bias, NEG_INF)
            pb = _softmax_rows(sg).astype(BF16)
            o_cmp.append(_dot(pb, cv2))
            imp = imp + _dot(pb, ovl_ref[...])

        sel = _select_topk(jnp.where(forced, jnp.inf, jnp.where(blk <= cur, imp, NEG_INF)), SEL_TOPK)
        sel_b = jnp.where(blk <= cur, sel, 0.0).astype(BF16)

        def sel_tile(m, carry, near):
            row0 = pl.multiple_of(Q_BLOCK * (qb - (KEY_TILE // Q_BLOCK) * m), Q_BLOCK)
            blk0 = (row0 - KEY_PAD) // SEL_BLOCK
            expand = jnp.where(e_row == blk0 + e_col, 1.0, 0.0).astype(BF16)
            tok = _dot(sel_b, expand) > 0.5
            sc = _dot_nt(q, skv_ref[0, pl.ds(row0, KEY_TILE), 0:HALF])
            if near:
                tok = tok & (col_t - KEY_PAD <= row_t)
                sc = sc + _near_bias(kv, m0_ref, m1_ref, far_ref, KEY_TILE - 2 * Q_BLOCK)
            else:
                sc = sc + _far_col(kv, far_ref)
            sc = _mask_rows(tok, sc)
            return _online_update(carry, sc, skv_ref[0, pl.ds(row0, KEY_TILE), HALF:KV_W])

        carry = sel_tile(0, _online_init(GROUP * Q_BLOCK), True)
        carry = lax.fori_loop(1, n_tiles, lambda m, c: sel_tile(m, c, False), carry)
        o_sel = _online_finish(carry)

        sw = _dot_nt(q, wkv_ref[0, pl.ds(pl.multiple_of(t0, Q_BLOCK), WINDOW + Q_BLOCK), 0:HALF])
        sw = sw + _near_bias(kv, m0_ref, m1_ref, far_ref, WINDOW - Q_BLOCK)
        pw = _softmax_rows(_mask_rows(win_valid, sw)).astype(BF16)
        o_win = _dot(pw, wkv_ref[0, pl.ds(pl.multiple_of(t0, Q_BLOCK), WINDOW + Q_BLOCK), HALF:KV_W])

        for g in range(GROUP):
            a = kv * GROUP + g
            rows = slice(g * Q_BLOCK, (g + 1) * Q_BLOCK)
            o = (gates[:, 3 * a:3 * a + 1] * o_cmp[g] + gates[:, 3 * a + 1:3 * a + 2] * o_sel[rows]
                 + gates[:, 3 * a + 2:3 * a + 3] * o_win[rows])
            o_ref[:, a * LANES:(a + 1) * LANES] = (o * lane_keep).astype(BF16)


def _overlap_padded(n_chunks, cw):
    n = np.arange(cw)[:, None] - CMP_PAD
    j = np.arange(LANES)[None, :]
    n_cmp = n_chunks - CMP_RATIO + 1
    m = (n >= 0) & (n < n_cmp) & (n * CMP_STRIDE < (j + 1) * SEL_BLOCK) & (n * CMP_STRIDE + CMP_BLOCK > j * SEL_BLOCK)
    return jnp.asarray(m.astype(np.float32), BF16)


def _nsa_prompt(nq, misc, ccmp, skvb, wkvb, tables, B, T):
    m0, m1, tc, far = tables
    n_qb = T // Q_BLOCK
    assert T % KEY_TILE == 0 and SEL_TOPK <= T // SEL_BLOCK <= LANES
    cw = ccmp.shape[1]
    ovl = _overlap_padded(T // CMP_STRIDE, cw)
    skv_p = jnp.pad(skvb.reshape(B, T, KV_W), ((0, 0), (KEY_PAD, 0), (0, 0)))
    wkv_p = jnp.pad(wkvb.reshape(B, T, KV_W), ((0, 0), (WINDOW, 0), (0, 0)))
    rowblk = lambda w: pl.BlockSpec((Q_BLOCK, w), lambda b, i: (b * n_qb + i, 0))
    seq = lambda a: pl.BlockSpec((1,) + a.shape[1:], lambda b, i: (b, 0, 0))
    return pl.pallas_call(
        _nsa_body,
        grid=(B, n_qb),
        in_specs=[rowblk(Q_EXT), rowblk(_MISC_W), seq(ccmp), seq(skv_p), seq(wkv_p),
                  _full_spec(ovl), _full_spec(tc), _full_spec(m0), _full_spec(m1), _full_spec(far)],
        out_specs=rowblk(Q_EXT),
        out_shape=jax.ShapeDtypeStruct((B * T, Q_EXT), BF16),
        compiler_params=pltpu.CompilerParams(dimension_semantics=("parallel", "arbitrary"),
                                             vmem_limit_bytes=VMEM_LIMIT),
        name="nsa_prompt",
    )(nq, misc, ccmp, skv_p, wkv_p, ovl, tc, m0, m1, far)


def _sortable(x):
    b = lax.bitcast_convert_type(x, I32)
    return b ^ ((b >> 31) & 0x7FFFFFFF)


def _kth_largest(count_ge, k, rows):
    zero = jnp.zeros((rows, 1), I32)
    ans = jnp.where(count_ge(zero) >= k, zero, jnp.full((rows, 1), INT_MIN, I32))

    def step(i, ans):
        cand = ans | jnp.left_shift(jnp.int32(1), 30 - i)
        return jnp.where(count_ge(cand) >= k, cand, ans)

    return lax.fori_loop(0, 31, step, ans)


def _tie_cut(count_eq_below, r, n_bits, rows):
    def step(i, c):
        cand = c | jnp.left_shift(jnp.int32(1), n_bits - 1 - i)
        return jnp.where(count_eq_below(cand) < r, cand, c)

    return lax.fori_loop(0, n_bits, step, jnp.zeros((rows, 1), I32))


def _dsa_body(k_top, q_ref, iq_ref, misc_ref, ik_ref, dkv_ref, m0_ref, m1_ref, far_ref, o_ref, key_scr):
    qb = pl.program_id(1)
    t0 = qb * Q_BLOCK
    tiles_per = KEY_TILE // Q_BLOCK
    n_tiles = qb // tiles_per + 1
    n_bits = int(key_scr.shape[1]).bit_length()
    row_t = lax.broadcasted_iota(I32, (Q_BLOCK, KEY_TILE), 0)
    col_t = lax.broadcasted_iota(I32, (Q_BLOCK, KEY_TILE), 1)

    def tile_row0(m):
        return pl.multiple_of(Q_BLOCK * (qb - tiles_per * m), Q_BLOCK)

    iq = jnp.concatenate([iq_ref[:, j * LANES:(j + 1) * LANES] for j in range(IDX_HEADS)], axis=0)
    iw = (misc_ref[:, _IW_COL:_IW_COL + IDX_HEADS] * IDX_HEADS ** -0.5).astype(BF16).astype(F32)

    def score_tile(m, _):
        row0 = tile_row0(m)
        x = _dot_nt(iq, ik_ref[0, pl.ds(row0, KEY_TILE), :])
        x = jnp.maximum(x, 0.0).astype(BF16).astype(F32)
        sc = x[0:Q_BLOCK] * iw[:, 0:1]
        for j in range(1, IDX_HEADS):
            sc = sc + x[j * Q_BLOCK:(j + 1) * Q_BLOCK] * iw[:, j:j + 1]
        pos = row0 - KEY_PAD + col_t
        sc = jnp.where(pos <= t0 + row_t, jnp.where(sc == 0.0, 0.0, sc), NEG_INF)
        key_scr[:, pl.ds(row0, KEY_TILE)] = jnp.where(pos >= 0, _sortable(sc), INT_MIN)
        return 0

    lax.fori_loop(0, n_tiles, score_tile, 0)

    def count(pred):
        def body(m, cnt):
            hit = jnp.where(pred(key_scr[:, pl.ds(tile_row0(m), KEY_TILE)], tile_row0(m) + col_t), 1.0, 0.0)
            return cnt + (hit[:, 0:LANES] + hit[:, LANES:2 * LANES]) + (hit[:, 2 * LANES:3 * LANES] + hit[:, 3 * LANES:])

        cnt = lax.fori_loop(0, n_tiles, body, jnp.zeros((Q_BLOCK, LANES), F32))
        return jnp.sum(cnt, axis=-1, keepdims=True)

    kf = float(k_top)
    v = _kth_largest(lambda cand: count(lambda key, col: key >= cand), kf, Q_BLOCK)
    v = jnp.maximum(v, INT_MIN + 1)
    need = kf - count(lambda key, col: key > v)
    cut = _tie_cut(lambda c: count(lambda key, col: (key == v) & (col < c)), need, n_bits, Q_BLOCK)

    qs = [_stack_heads(q_ref, kv) for kv in range(DSA_KV_HEADS)]

    def attn_tile(m, carry, near):
        row0 = tile_row0(m)
        key = key_scr[:, pl.ds(row0, KEY_TILE)]
        keep = (key > v) | ((key == v) & (row0 + col_t <= cut))
        if near:
            keep = keep & (col_t - KEY_PAD <= row_t)
        out = []
        for kv in range(DSA_KV_HEADS):
            sc = _dot_nt(qs[kv], dkv_ref[0, pl.ds(row0, KEY_TILE), 0:HALF])
            if near:
                sc = sc + _near_bias(kv, m0_ref, m1_ref, far_ref, KEY_TILE - 2 * Q_BLOCK)
            else:
                sc = sc + _far_col(kv, far_ref)
            sc = _mask_rows(keep, sc)
            out.append(_online_update(carry[kv], sc, dkv_ref[0, pl.ds(row0, KEY_TILE), HALF:KV_W]))
        return tuple(out)

    carry = attn_tile(0, tuple(_online_init(GROUP * Q_BLOCK) for _ in range(DSA_KV_HEADS)), True)
    carry = lax.fori_loop(1, n_tiles, lambda m, c: attn_tile(m, c, False), carry)
    for kv in range(DSA_KV_HEADS):
        o = _online_finish(carry[kv])
        lane_keep = (lax.broadcasted_iota(I32, (Q_BLOCK, LANES), 1) // HEAD_DIM == kv).astype(F32)
        for g in range(GROUP):
            a = kv * GROUP + g
            o_ref[:, a * LANES:(a + 1) * LANES] = (o[g * Q_BLOCK:(g + 1) * Q_BLOCK] * lane_keep).astype(BF16)


def _dsa_prompt(dq, iq, misc, ikx, dkvb, tables, B, T):
    m0, m1, _, far = tables
    n_qb = T // Q_BLOCK
    k_top = min(DSA_TOPK_MAX, T // 4)
    ik_p = jnp.pad(ikx.reshape(B, T, LANES), ((0, 0), (KEY_PAD, 0), (0, 0)))
    dkv_p = jnp.pad(dkvb.reshape(B, T, KV_W), ((0, 0), (KEY_PAD, 0), (0, 0)))
    rowblk = lambda w: pl.BlockSpec((Q_BLOCK, w), lambda b, i: (b * n_qb + i, 0))
    seq = lambda a: pl.BlockSpec((1,) + a.shape[1:], lambda b, i: (b, 0, 0))
    return pl.pallas_call(
        functools.partial(_dsa_body, k_top),
        grid=(B, n_qb),
        in_specs=[rowblk(Q_EXT), rowblk(IDX_HEADS * LANES), rowblk(_MISC_W), seq(ik_p), seq(dkv_p),
                  _full_spec(m0), _full_spec(m1), _full_spec(far)],
        out_specs=rowblk(Q_EXT),
        out_shape=jax.ShapeDtypeStruct((B * T, Q_EXT), BF16),
        scratch_shapes=[pltpu.VMEM((Q_BLOCK, T + KEY_PAD), I32)],
        compiler_params=pltpu.CompilerParams(dimension_semantics=("parallel", "arbitrary"),
                                             vmem_limit_bytes=VMEM_LIMIT),
        name="dsa_prompt",
    )(dq, iq, misc, ik_p, dkv_p, m0, m1, far)


def _ext_rows(w):
    d = w.shape[1]
    w4 = w.reshape(N_HEADS, 1, HEAD_DIM, d)
    sel = jnp.asarray(np.eye(2, dtype=np.float32)[[a // GROUP for a in range(N_HEADS)]], w.dtype)
    return (w4 * sel[:, :, None, None]).reshape(Q_EXT, d)


def _merge_body(x_ref, on_ref, od_ref, g_ref, wmg_ref, wa_ref, wb_ref, wo_ref, out_ref):
    x = x_ref[...]
    h = _rms(x, g_ref[...]).astype(BF16)
    m = jax.nn.sigmoid(_dot(h, wmg_ref[:, :D_MODEL])) * _dot(on_ref[...], wa_ref[...])
    m = m + jax.nn.sigmoid(_dot(h, wmg_ref[:, D_MODEL:])) * _dot(od_ref[...], wb_ref[...])
    out_ref[...] = x + _dot(m.astype(BF16), wo_ref[...])


def _merge(x2d, o_nsa, o_dsa, g, w_mg, w_a, w_b, w_o):
    rows = x2d.shape[0]
    tm = min(ROW_TILE, rows)
    row = lambda w: pl.BlockSpec((tm, w), lambda i: (i, 0))
    return pl.pallas_call(
        _merge_body,
        grid=(rows // tm,),
        in_specs=[row(D_MODEL), row(Q_EXT), row(Q_EXT), _full_spec(g), _full_spec(w_mg), _full_spec(w_a),
                  _full_spec(w_b), _full_spec(w_o)],
        out_specs=row(D_MODEL),
        out_shape=jax.ShapeDtypeStruct((rows, D_MODEL), F32),
        compiler_params=pltpu.CompilerParams(dimension_semantics=("parallel",), vmem_limit_bytes=VMEM_LIMIT),
        name="merge",
    )(x2d, o_nsa, o_dsa, g, w_mg, w_a, w_b, w_o)


def _ffn_body(n_chunks, x_ref, g_ref, gf_ref, wg_ref, wu_ref, wd_ref, out_ref):
    x = x_ref[...]
    h = _rms(x, g_ref[...]).astype(BF16)
    fc = wg_ref.shape[1] // n_chunks
    acc = x
    for c in range(n_chunks):
        gate = _dot(h, wg_ref[:, c * fc:(c + 1) * fc])
        up = _dot(h, wu_ref[:, c * fc:(c + 1) * fc])
        act = (gate * jax.nn.sigmoid(gate) * up).astype(BF16)
        acc = acc + _dot(act, wd_ref[c * fc:(c + 1) * fc, :])
    out_ref[...] = _rms(acc, gf_ref[...])


def _ffn(x2d, g, g_final, w_gate, w_up, w_down):
    rows = x2d.shape[0]
    tm = min(ROW_TILE, rows)
    d_ff = w_gate.shape[1]
    n_chunks = 2 if d_ff % (2 * LANES) == 0 else 1
    row = lambda w: pl.BlockSpec((tm, w), lambda i: (i, 0))
    return pl.pallas_call(
        functools.partial(_ffn_body, n_chunks),
        grid=(rows // tm,),
        in_specs=[row(D_MODEL), _full_spec(g), _full_spec(g_final), _full_spec(w_gate), _full_spec(w_up),
                  _full_spec(w_down)],
        out_specs=row(D_MODEL),
        out_shape=jax.ShapeDtypeStruct((rows, D_MODEL), F32),
        compiler_params=pltpu.CompilerParams(dimension_semantics=("parallel",), vmem_limit_bytes=VMEM_LIMIT),
        name="ffn",
    )(x2d, g, g_final, w_gate, w_up, w_down)


S_ROWS = 16


def _page_copy(pool_ref, page, buf_ref, slot, p, sem_ref):
    rows = pool_ref.shape[1]
    return pltpu.make_async_copy(pool_ref.at[page], buf_ref.at[slot, pl.ds(p * rows, rows)], sem_ref.at[slot])


def _paged_step(pt_ref, pools):
    b = pl.program_id(0)
    n_pages = pt_ref.shape[1]
    slot = b % 2

    def fetch(seq, dst):
        for pool_ref, buf_ref, sem_ref in pools:
            for p in range(n_pages):
                _page_copy(pool_ref, pt_ref[seq, p], buf_ref, dst, p, sem_ref).start()

    @pl.when(b == 0)
    def _():
        fetch(0, 0)

    @pl.when(b + 1 < pl.num_programs(0))
    def _():
        fetch(b + 1, 1 - slot)

    for pool_ref, buf_ref, sem_ref in pools:
        for p in range(n_pages):
            _page_copy(pool_ref, 0, buf_ref, slot, p, sem_ref).wait()
    return slot


def _sum_over_heads(x):
    x = x + pltpu.roll(x, S_ROWS // 2, axis=0)
    return x + pltpu.roll(x, S_ROWS // 4, axis=0)


def _lane_keep(kv, rows):
    return (lax.broadcasted_iota(I32, (rows, LANES), 1) // HEAD_DIM == kv).astype(F32)


def _sample_cmp_body(pt_ref, pool_ref, q_ref, pex_ref, w1_ref, w2_ref, b1_ref, bias_ref, ovl_ref, selval_ref,
                     ocmp_ref, sel_ref, buf, sem):
    slot = _paged_step(pt_ref, [(pool_ref, buf, sem)])
    o0, o1 = _compress_chunks(buf[slot], pex_ref[...], w1_ref, w2_ref, b1_ref)
    ck2 = o0.astype(BF16)
    cv2 = o1.astype(BF16)
    sv = selval_ref[...]
    for kv in range(NSA_KV_HEADS):
        pb = _softmax_rows(_dot_nt(q_ref[0, kv], ck2) + bias_ref[kv]).astype(BF16)
        ocmp_ref[0, kv] = _dot(pb, cv2)
        imp = _sum_over_heads(_dot(pb, ovl_ref[...]))
        sel = _select_topk(jnp.where(sv == 0.0, imp, sv), SEL_TOPK)
        sel_ref[0, kv] = jnp.where(sv == NEG_INF, 0.0, sel)


def _sample_sel_body(pt_ref, pool_ref, q_ref, selm_ref, ocmp_ref, ng_ref, snew_ref, swin_ref, wnew_ref,
                     sbias_ref, snbias_ref, wbias_ref, wnbias_ref, e_ref, enew_ref, o_ref, buf, sem):
    slot = _paged_step(pt_ref, [(pool_ref, buf, sem)])
    k2 = buf[slot, :, 0:HALF].astype(BF16)
    v2 = buf[slot, :, HALF:KV_W].astype(BF16)
    kn = snew_ref[0, :, 0:HALF].astype(BF16)
    vn = snew_ref[0, :, HALF:KV_W].astype(BF16)
    kw = swin_ref[0, :, 0:HALF].astype(BF16)
    vw = swin_ref[0, :, HALF:KV_W].astype(BF16)
    kwn = wnew_ref[0, :, 0:HALF].astype(BF16)
    vwn = wnew_ref[0, :, HALF:KV_W].astype(BF16)
    for kv in range(NSA_KV_HEADS):
        q = q_ref[0, kv]
        selm = selm_ref[0, kv].astype(BF16)
        tok = _dot(selm, e_ref[...]) > 0.5
        tokn = _dot(selm, enew_ref[...]) > 0.5
        p, pn = _softmax_pair(jnp.where(tok, _dot_nt(q, k2) + sbias_ref[kv], NEG_INF),
                              jnp.where(tokn, _dot_nt(q, kn) + snbias_ref[kv], NEG_INF))
        o_sel = _dot(p.astype(BF16), v2) + _dot(pn.astype(BF16), vn)
        pw, pwn = _softmax_pair(_dot_nt(q, kw) + wbias_ref[kv], _dot_nt(q, kwn) + wnbias_ref[kv])
        o_win = _dot(pw.astype(BF16), vw) + _dot(pwn.astype(BF16), vwn)
        g = jax.nn.sigmoid(ng_ref[0, kv])
        o = g[:, 0:1] * ocmp_ref[0, kv] + g[:, 1:2] * o_sel + g[:, 2:3] * o_win
        o_ref[0, kv] = (o * _lane_keep(kv, S_ROWS)).astype(BF16)


def _sample_dsa_body(k_top, pt_ref, ipool_ref, kpool_ref, iq_ref, iw_ref, q_ref, inew_ref, knew_ref,
                     dbias_ref, dnbias_ref, newvalid_ref, o_ref, ibuf, kbuf, isem, ksem):
    slot = _paged_step(pt_ref, [(ipool_ref, ibuf, isem), (kpool_ref, kbuf, ksem)])
    n_keys = ibuf.shape[1]
    iq = iq_ref[0]
    iw = iw_ref[0].astype(BF16).astype(F32)

    def index_score(ik):
        x = jnp.maximum(_dot_nt(iq, ik.astype(BF16)), 0.0).astype(BF16).astype(F32)
        sc = _sum_over_heads(x * iw)
        return jnp.where(sc == 0.0, 0.0, sc)

    key_c = _sortable(index_score(ibuf[slot]))
    nv = newvalid_ref[...]
    sc_n = index_score(inew_ref[0])
    key_n = jnp.where(nv == 0.0, _sortable(sc_n), jnp.where(nv == NEG_INF, _sortable(nv), INT_MIN))
    col_c = lax.broadcasted_iota(I32, key_c.shape, 1)
    col_n = n_keys + lax.broadcasted_iota(I32, key_n.shape, 1)

    def count(pred):
        one = lambda hit: jnp.sum(jnp.where(hit, 1.0, 0.0), axis=-1, keepdims=True)
        return one(pred(key_c, col_c)) + one(pred(key_n, col_n))

    kf = float(k_top)
    v = _kth_largest(lambda cand: count(lambda key, col: key >= cand), kf, S_ROWS)
    v = jnp.maximum(v, INT_MIN + 1)
    need = kf - count(lambda key, col: key > v)
    n_bits = int(n_keys + S_ROWS).bit_length()
    cut = _tie_cut(lambda c: count(lambda key, col: (key == v) & (col < c)), need, n_bits, S_ROWS)
    keep_c = (key_c > v) | ((key_c == v) & (col_c <= cut))
    keep_n = (key_n > v) | ((key_n == v) & (col_n <= cut))

    k2 = kbuf[slot, :, 0:HALF].astype(BF16)
    v2 = kbuf[slot, :, HALF:KV_W].astype(BF16)
    kn = knew_ref[0, :, 0:HALF].astype(BF16)
    vn = knew_ref[0, :, HALF:KV_W].astype(BF16)
    for kv in range(DSA_KV_HEADS):
        q = q_ref[0, kv]
        p, pn = _softmax_pair(jnp.where(keep_c, _dot_nt(q, k2) + dbias_ref[kv], NEG_INF),
                              jnp.where(keep_n, _dot_nt(q, kn) + dnbias_ref[kv], NEG_INF))
        o = _dot(p.astype(BF16), v2) + _dot(pn.astype(BF16), vn)
        o_ref[0, kv] = (o * _lane_keep(kv, S_ROWS)).astype(BF16)


def _sample_tables(tab_nsa, tab_dsa, past, dt, w_buf):
    n_chunks = past // CMP_STRIDE
    n_cmp = n_chunks - CMP_RATIO + 1
    qpos = past + (np.arange(S_ROWS) % dt)
    t_of = (np.arange(S_ROWS) % dt)[:, None]

    def per_head(tab, dist, valid):
        idx = _np_bucket(dist)
        out = []
        for kv in range(2):
            rows = [tab[idx[g * dt:(g + 1) * dt], kv * GROUP + g] for g in range(GROUP)]
            out.append(jnp.concatenate(rows, axis=0))
        return jnp.where(jnp.asarray(valid)[None], jnp.stack(out), NEG_INF)

    n = np.arange(n_chunks)[None, :]
    d = qpos[:, None] - (CMP_STRIDE * n + CMP_BLOCK - 1)
    cmp_bias = per_head(tab_nsa, d, (d >= 0) & (n < n_cmp))
    s = np.arange(past)[None, :]
    d = qpos[:, None] - s
    sel_bias = per_head(tab_nsa, d, d >= 0)
    dsa_bias = per_head(tab_dsa, d, d >= 0)
    u = np.arange(S_ROWS)[None, :]
    new_ok = (u <= t_of) & (u < dt)
    new_bias_nsa = per_head(tab_nsa, t_of - u, new_ok)
    new_bias_dsa = per_head(tab_dsa, t_of - u, new_ok)
    i = np.arange(w_buf)[None, :]
    d = qpos[:, None] - (past - w_buf + i)
    win_bias = per_head(tab_nsa, d, (d >= 0) & (d < WINDOW) & (past - w_buf + i >= 0))
    new_valid = np.where(new_ok, 0.0, np.where(u < dt, -np.inf, 1.0)).astype(np.float32)

    n_sel = -(-(past + dt) // SEL_BLOCK)
    sel_w = -(-n_sel // LANES) * LANES
    j = np.arange(sel_w)[None, :]
    cur = (qpos // SEL_BLOCK)[:, None]
    forced = (j == 0) | (j == cur) | (j == cur - 1)
    sel_val = np.where(forced, np.inf, np.where((j <= cur) & (j < n_sel), 0.0, -np.inf)).astype(np.float32)
    nn = np.arange(n_chunks)[:, None]
    ovl = ((nn < n_cmp) & (nn * CMP_STRIDE < (j + 1) * SEL_BLOCK) & (nn * CMP_STRIDE + CMP_BLOCK > j * SEL_BLOCK))
    jj = np.arange(sel_w)[:, None]
    expand = jj == (np.arange(past)[None, :] // SEL_BLOCK)
    expand_new = (jj == ((past + u) // SEL_BLOCK)) & (u < dt)
    b16 = lambda m: jnp.asarray(m.astype(np.float32), BF16)
    return dict(cmp_bias=cmp_bias, sel_bias=sel_bias, dsa_bias=dsa_bias, new_bias_nsa=new_bias_nsa,
                new_bias_dsa=new_bias_dsa, win_bias=win_bias, new_valid=jnp.asarray(new_valid),
                sel_val=jnp.asarray(sel_val), ovl=b16(ovl), expand=b16(expand), expand_new=b16(expand_new))


def _sample_mixers(ps, DB, DT, c_cmp, c_sel, c_dkv, c_idx, s_win, page_table, cw, tab_nsa, tab_dsa):
    w1, w2, pex, b1 = cw
    n_pool, page = c_cmp.shape[:2]
    n_pages = page_table.shape[1]
    past = n_pages * page
    w_buf = s_win.shape[1]
    assert DT * GROUP == S_ROWS and past % KEY_TILE == 0
    tb = _sample_tables(tab_nsa, tab_dsa, past, DT, w_buf)
    k_top = min(DSA_TOPK_MAX, (past + DT) // 4)
    sel_w = tb["sel_val"].shape[1]

    rows = lambda a, w: jnp.transpose(a.reshape(DB, DT, 2, GROUP, w), (0, 2, 3, 1, 4)).reshape(DB, 2, S_ROWS, w)
    nq = rows(ps["nq"], LANES)
    dq = rows(ps["dq"], LANES)
    ng = rows(ps["misc"][:, _NG_COL:_NG_COL + 3 * NSA_HEADS], 3)
    iq = jnp.transpose(ps["iq"].reshape(DB, DT, IDX_HEADS, LANES)[..., :IDX_DIM], (0, 2, 1, 3)).reshape(
        DB, S_ROWS, IDX_DIM)
    iw = jnp.transpose(ps["misc"][:, _IW_COL:_IW_COL + IDX_HEADS].reshape(DB, DT, IDX_HEADS) * IDX_HEADS ** -0.5,
                       (0, 2, 1)).reshape(DB, S_ROWS, 1)
    new16 = lambda a: jnp.pad(a.reshape(DB, DT, a.shape[-1]), ((0, 0), (0, S_ROWS - DT), (0, 0)))
    s_new, w_new, d_new, i_new = new16(ps["skv"]), new16(ps["wkv"]), new16(ps["dkv"]), new16(ps["ik"])

    seq = lambda a: pl.BlockSpec((1,) + a.shape[1:], lambda b, pt: (b,) + (0,) * (a.ndim - 1))
    const = lambda a: pl.BlockSpec(a.shape, lambda b, pt: (0,) * a.ndim)
    hbm = pl.BlockSpec(memory_space=pl.ANY)
    params = pltpu.CompilerParams(dimension_semantics=("arbitrary",), vmem_limit_bytes=VMEM_LIMIT)
    grp = lambda w, dt: jax.ShapeDtypeStruct((DB, 2, S_ROWS, w), dt)
    out_grp = lambda w: pl.BlockSpec((1, 2, S_ROWS, w), lambda b, pt: (b, 0, 0, 0))

    cmp_pool = c_cmp.reshape(n_pool, page // CMP_STRIDE, CMP_STRIDE * KV_W)
    ins = [nq, pex, w1, w2, b1, tb["cmp_bias"], tb["ovl"], tb["sel_val"]]
    o_cmp, selm = pl.pallas_call(
        _sample_cmp_body,
        grid_spec=pltpu.PrefetchScalarGridSpec(
            num_scalar_prefetch=1, grid=(DB,),
            in_specs=[hbm, seq(nq)] + [const(a) for a in ins[1:]],
            out_specs=[out_grp(LANES), out_grp(sel_w)],
            scratch_shapes=[pltpu.VMEM((2, past // CMP_STRIDE, CMP_STRIDE * KV_W), F32),
                            pltpu.SemaphoreType.DMA((2,))]),
        out_shape=[grp(LANES, F32), grp(sel_w, F32)],
        compiler_params=params, name="sample_cmp",
    )(page_table, cmp_pool, *ins)

    sel_pool = c_sel.reshape(n_pool, page, KV_W)
    s_win3 = s_win.reshape(DB, w_buf, KV_W)
    ins = [nq, selm, o_cmp, ng, s_new, s_win3, w_new]
    consts = [tb["sel_bias"], tb["new_bias_nsa"], tb["win_bias"], tb["new_bias_nsa"], tb["expand"], tb["expand_new"]]
    o_nsa = pl.pallas_call(
        _sample_sel_body,
        grid_spec=pltpu.PrefetchScalarGridSpec(
            num_scalar_prefetch=1, grid=(DB,),
            in_specs=[hbm] + [seq(a) for a in ins] + [const(a) for a in consts],
            out_specs=out_grp(LANES),
            scratch_shapes=[pltpu.VMEM((2, past, KV_W), F32), pltpu.SemaphoreType.DMA((2,))]),
        out_shape=grp(LANES, BF16),
        compiler_params=params, name="sample_sel",
    )(page_table, sel_pool, *ins, *consts)

    idx_pool = c_idx.reshape(n_pool, page, IDX_DIM)
    dkv_pool = c_dkv.reshape(n_pool, page, KV_W)
    ins = [iq, iw, dq, i_new, d_new]
    consts = [tb["dsa_bias"], tb["new_bias_dsa"], tb["new_valid"]]
    o_dsa = pl.pallas_call(
        functools.partial(_sample_dsa_body, k_top),
        grid_spec=pltpu.PrefetchScalarGridSpec(
            num_scalar_prefetch=1, grid=(DB,),
            in_specs=[hbm, hbm] + [seq(a) for a in ins] + [const(a) for a in consts],
            out_specs=out_grp(LANES),
            scratch_shapes=[pltpu.VMEM((2, past, IDX_DIM), F32), pltpu.VMEM((2, past, KV_W), F32),
                            pltpu.SemaphoreType.DMA((2,)), pltpu.SemaphoreType.DMA((2,))]),
        out_shape=grp(LANES, BF16),
        compiler_params=params, name="sample_dsa",
    )(page_table, idx_pool, dkv_pool, *ins, *consts)

    unrows = lambda o: jnp.transpose(o.reshape(DB, 2, GROUP, DT, LANES), (0, 3, 1, 2, 4)).reshape(DB * DT, Q_EXT)
    return unrows(o_nsa), unrows(o_dsa)


def kernel(x_prompt, x_sample, cache_nsa_cmp, cache_nsa_sel, cache_dsa_kv, cache_dsa_idx, state_nsa_win, page_table, norm_mix, w_in, cmp_pe, cmp_w1, cmp_b1, cmp_w2, rel_bias, w_branch_nsa, w_branch_dsa, w_out, norm_ffn, w_gate, w_up, w_down, norm_final):
    assert norm_mix.shape[0] == 1, "single trunk layer"
    B, T, _ = x_prompt.shape
    DB, DT, _ = x_sample.shape
    tab_nsa = rel_bias[:, :NSA_HEADS]
    tab_dsa = rel_bias[:, NSA_HEADS:]
    l = 0
    g_mix = norm_mix[l][None]
    g_ffn = norm_ffn[l][None]
    g_fin = norm_final[None]
    w_pack, w_mg = _pack_w_in(w_in[l])
    w_a = _ext_rows(w_branch_nsa[l]).astype(BF16)
    w_b = _ext_rows(w_branch_dsa[l]).astype(BF16)
    w_o = w_out[l].astype(BF16)
    wg, wu, wd = (w.astype(BF16) for w in (w_gate[l], w_up[l], w_down[l]))
    cw = _pack_cmp_weights(cmp_pe[l], cmp_w1[l], cmp_b1[l], cmp_w2[l])

    def tail(x2d, o_n, o_d):
        x1 = _merge(x2d, o_n, o_d, g_mix, w_mg, w_a, w_b, w_o)
        return _ffn(x1, g_ffn, g_fin, wg, wu, wd)

    xp = x_prompt.reshape(B * T, D_MODEL)
    pp = _project(xp, g_mix, w_pack)
    ccmp = _compress_prompt(pp["ckv"].reshape(B, T, KV_W), cw)
    o_n = _nsa_prompt(pp["nq"], pp["misc"], ccmp, pp["skvb"], pp["wkvb"], _prompt_tables(tab_nsa), B, T)
    o_d = _dsa_prompt(pp["dq"], pp["iq"], pp["misc"], pp["ikx"], pp["dkvb"], _prompt_tables(tab_dsa), B, T)
    y_prompt = tail(xp, o_n, o_d).reshape(B, T, D_MODEL)

    xs = x_sample.reshape(DB * DT, D_MODEL)
    ps = _project(xs, g_mix, w_pack)
    o_n, o_d = _sample_mixers(ps, DB, DT, cache_nsa_cmp[l], cache_nsa_sel[l], cache_dsa_kv[l], cache_dsa_idx[l],
                              state_nsa_win[l], page_table, cw, tab_nsa, tab_dsa)
    y_sample = tail(xs, o_n, o_d).reshape(DB, DT, D_MODEL)

    kv6 = lambda a, b, t: a.reshape(1, b, t, 2, NSA_KV_HEADS, HEAD_DIM)
    win_s = jnp.concatenate([state_nsa_win[l], kv6(ps["wkv"], DB, DT)[0].astype(state_nsa_win.dtype)], axis=1)[:, DT:]
    w_keep = min(WINDOW, T)
    return (y_prompt, y_sample,
            kv6(pp["ckv"], B, T), kv6(ps["ckv"], DB, DT),
            kv6(pp["skv"], B, T), kv6(ps["skv"], DB, DT),
            kv6(pp["dkv"], B, T), kv6(ps["dkv"], DB, DT),
            pp["ik"].reshape(1, B, T, IDX_DIM), ps["ik"].reshape(1, DB, DT, IDX_DIM),
            kv6(pp["wkv"], B, T)[:, :, T - w_keep:], win_s[None])
```

```python
import functools
import math

import jax
import jax.numpy as jnp
import numpy as np
from jax import lax
from jax.experimental import pallas as pl
from jax.experimental.pallas import tpu as pltpu

F32 = jnp.float32
BF16 = jnp.bfloat16
I32 = jnp.int32

D_MODEL = 1024
HEAD_DIM = 64
NSA_HEADS = 8
NSA_KV_HEADS = 2
NSA_GROUP = NSA_HEADS // NSA_KV_HEADS
CMP_BLOCK = 32
CMP_STRIDE = 16
CMP_RATIO = CMP_BLOCK // CMP_STRIDE
CMP_HIDDEN = 2 * HEAD_DIM
SEL_BLOCK = 64
SEL_TOPK = 16
WINDOW = 512
DSA_HEADS = 8
DSA_KV_HEADS = 2
DSA_GROUP = DSA_HEADS // DSA_KV_HEADS
IDX_HEADS = 4
IDX_DIM = 64
DSA_TOPK_MAX = 256
N_BUCKETS = 32
MAX_DISTANCE = 128
Q_BLOCK = 128
EPS = 1e-6
SCALE = HEAD_DIM ** -0.5

LANES = 128
KV_W = 2 * NSA_KV_HEADS * HEAD_DIM
HALF = KV_W // 2
N_HEADS = NSA_HEADS
GROUP = NSA_GROUP
Q_EXT = N_HEADS * LANES
KEY_TILE = 512
KEY_PAD = KEY_TILE - Q_BLOCK
CMP_PAD = 128
CMP_NEAR = 256
PROJ_WIDTHS = (NSA_HEADS * HEAD_DIM, KV_W, KV_W, KV_W, 3 * NSA_HEADS, DSA_HEADS * HEAD_DIM, KV_W,
               IDX_HEADS * IDX_DIM, IDX_DIM, IDX_HEADS, 2 * D_MODEL)
PROJ_OFFS = tuple(int(v) for v in np.cumsum((0,) + PROJ_WIDTHS))
INT_MIN = -2 ** 31
NEG_INF = float("-inf")

VMEM_LIMIT = 56 * 1024 * 1024
ROW_TILE = 512


def _rms(x, g):
    return x * lax.rsqrt(jnp.mean(x * x, axis=-1, keepdims=True) + EPS) * g


def _dot(a, b):
    return jnp.dot(a, b, preferred_element_type=F32)


def _dot_nt(a, b):
    return lax.dot_general(a, b, (((1,), (1,)), ((), ())), preferred_element_type=F32)


def _full_spec(a):
    nd = a.ndim
    return pl.BlockSpec(a.shape, lambda *_: (0,) * nd)


_MISC_W = LANES
_IW_COL = 0
_NG_COL = IDX_HEADS
_PROJ_OUT = (
    ("nq", Q_EXT, BF16), ("dq", Q_EXT, BF16), ("iq", IDX_HEADS * LANES, BF16), ("ikx", LANES, BF16),
    ("ckv", KV_W, F32), ("skv", KV_W, F32), ("wkv", KV_W, F32), ("dkv", KV_W, F32), ("misc", _MISC_W, F32))


def _ext_cols(w, n_heads, kv_of_head):
    d = w.shape[0]
    w4 = w.reshape(d, n_heads, 1, HEAD_DIM)
    sel = jnp.asarray(np.eye(2, dtype=np.float32)[[kv_of_head(a) for a in range(n_heads)]], w.dtype)
    return (w4 * sel[None, :, :, None]).reshape(d, n_heads * LANES)


def _pack_w_in(w_in):
    o = PROJ_OFFS
    seg = lambda i: w_in[:, o[i]:o[i + 1]]
    zeros = lambda n: jnp.zeros((D_MODEL, n), w_in.dtype)
    cols = [_ext_cols(seg(0), N_HEADS, lambda a: a // GROUP),
            _ext_cols(seg(5), N_HEADS, lambda a: a // GROUP),
            _ext_cols(seg(7), IDX_HEADS, lambda a: 0),
            seg(8), zeros(LANES - IDX_DIM),
            seg(1), seg(2), seg(3), seg(6),
            seg(9), seg(4), zeros(_MISC_W - IDX_HEADS - 3 * NSA_HEADS)]
    return jnp.concatenate(cols, axis=1).astype(BF16), seg(10).astype(BF16)


def _proj_body(x_ref, g_ref, w_ref, nq_ref, dq_ref, iq_ref, ikx_ref, ckv_ref, skv_ref, wkv_ref, dkv_ref, misc_ref,
               ik_ref, skvb_ref, wkvb_ref, dkvb_ref):
    h = _rms(x_ref[...], g_ref[...]).astype(BF16)
    c = 0

    def mm(w):
        nonlocal c
        r = _dot(h, w_ref[:, c:c + w])
        c += w
        return r

    nq_ref[...] = (mm(Q_EXT) * SCALE).astype(BF16)
    dq_ref[...] = (mm(Q_EXT) * SCALE).astype(BF16)
    iq_ref[...] = mm(IDX_HEADS * LANES).astype(BF16)
    ik = mm(LANES)
    ikx_ref[...] = ik.astype(BF16)
    ik_ref[...] = ik[:, :IDX_DIM]
    ckv_ref[...] = mm(KV_W)
    skv = mm(KV_W)
    skv_ref[...] = skv
    skvb_ref[...] = skv.astype(BF16)
    wkv = mm(KV_W)
    wkv_ref[...] = wkv
    wkvb_ref[...] = wkv.astype(BF16)
    dkv = mm(KV_W)
    dkv_ref[...] = dkv
    dkvb_ref[...] = dkv.astype(BF16)
    misc_ref[...] = mm(_MISC_W)


def _project(x2d, g, w_pack):
    rows = x2d.shape[0]
    tm = min(ROW_TILE, rows)
    row = lambda w: pl.BlockSpec((tm, w), lambda i: (i, 0))
    outs = list(_PROJ_OUT) + [("ik", IDX_DIM, F32), ("skvb", KV_W, BF16), ("wkvb", KV_W, BF16), ("dkvb", KV_W, BF16)]
    res = pl.pallas_call(
        _proj_body,
        grid=(rows // tm,),
        in_specs=[row(D_MODEL), _full_spec(g), _full_spec(w_pack)],
        out_specs=[row(w) for _, w, _ in outs],
        out_shape=[jax.ShapeDtypeStruct((rows, w), d) for _, w, d in outs],
        compiler_params=pltpu.CompilerParams(dimension_semantics=("parallel",), vmem_limit_bytes=VMEM_LIMIT),
        name="proj",
    )(x2d, g, w_pack)
    return {name: r for (name, _, _), r in zip(outs, res)}


def _np_bucket(dist):
    d = np.maximum(np.asarray(dist, np.int64), 0)
    max_exact = N_BUCKETS // 2
    logd = np.log(np.maximum(d, 1).astype(np.float64) / max_exact) / math.log(MAX_DISTANCE / max_exact)
    large = np.minimum(max_exact + (logd * (N_BUCKETS - max_exact)).astype(np.int64), N_BUCKETS - 1)
    return np.where(d < max_exact, d, large).astype(np.int32)


_FAR_DIST = 129
assert int(_np_bucket(_FAR_DIST - 16).min()) == N_BUCKETS - 1


def _prompt_tables(tab):
    rel = tab - tab[N_BUCKETS - 1][None, :]

    def lookup(dist):
        onehot = jnp.asarray(_np_bucket(dist))[..., None] == jnp.arange(N_BUCKETS)
        return jnp.einsum('ijb,bh->hij', onehot.astype(F32), rel, precision=lax.Precision.HIGHEST)

    i = np.arange(Q_BLOCK)[:, None]
    j = np.arange(Q_BLOCK)[None, :]
    c = np.arange(CMP_NEAR)[None, :]
    return lookup(i - j), lookup(Q_BLOCK + i - j), lookup(i - CMP_STRIDE * (c - CMP_PAD) - (CMP_BLOCK - 1))


def _pack_cmp_weights(cmp_pe, cmp_w1, cmp_b1, cmp_w2):
    eye = jnp.eye(NSA_KV_HEADS, dtype=cmp_w1.dtype)
    w1r = cmp_w1.reshape(2, CMP_RATIO, CMP_STRIDE, HEAD_DIM, CMP_HIDDEN)
    w1 = jnp.einsum('crldf,hk->clhdrkf', w1r, eye).reshape(
        2, CMP_STRIDE * HALF, CMP_RATIO * NSA_KV_HEADS * CMP_HIDDEN)
    w2 = jnp.einsum('cfd,hk->chfkd', cmp_w2, eye).reshape(2, NSA_KV_HEADS * CMP_HIDDEN, HALF)
    pe = cmp_pe.reshape(2, CMP_RATIO, CMP_STRIDE, HEAD_DIM)
    pex = jnp.broadcast_to(jnp.transpose(pe, (1, 2, 0, 3))[:, :, :, None, :],
                           (CMP_RATIO, CMP_STRIDE, 2, NSA_KV_HEADS, HEAD_DIM)).reshape(CMP_RATIO, CMP_STRIDE * KV_W)
    pex = jnp.concatenate([pex, jnp.zeros((8 - CMP_RATIO, CMP_STRIDE * KV_W), pex.dtype)], axis=0)
    b1 = jnp.concatenate([cmp_b1] * NSA_KV_HEADS, axis=-1)[:, None, :]
    return w1.astype(BF16), w2.astype(BF16), pex, b1


def _compress_chunks(chunk_cols, nc, pex, w1_ref, w2_ref, b1_ref):
    hw = NSA_KV_HEADS * CMP_HIDDEN
    outs = []
    for c in range(2):
        cat = lambda parts: jnp.concatenate(parts, axis=1).astype(BF16)
        hr = _dot(cat([chunk_cols(l, c) for l in range(CMP_STRIDE)]), w1_ref[c])
        hp = _dot(cat([pex[:, l * KV_W + c * HALF:l * KV_W + (c + 1) * HALF] for l in range(CMP_STRIDE)]), w1_ref[c])
        bias = b1_ref[c] + hp[0:1, 0:hw] + hp[1:2, hw:2 * hw]
        hsum = hr[:, 0:hw] + pltpu.roll(hr[:, hw:2 * hw], nc - 1, axis=0) + bias
        outs.append(_dot(jax.nn.gelu(hsum, approximate=True).astype(BF16), w2_ref[c]))
    return outs


def _compress_body(x_ref, pex_ref, w1_ref, w2_ref, b1_ref, out_ref):
    nc = x_ref.shape[1]
    chunk_cols = lambda l, c: x_ref[0, :, l * KV_W + c * HALF:l * KV_W + (c + 1) * HALF]
    o0, o1 = _compress_chunks(chunk_cols, nc, pex_ref[...], w1_ref, w2_ref, b1_ref)
    out_ref[0, 0:CMP_PAD, :] = jnp.zeros((CMP_PAD, KV_W), BF16)
    out_ref[0, CMP_PAD + nc:, :] = jnp.zeros((CMP_PAD, KV_W), BF16)
    out_ref[0, CMP_PAD:CMP_PAD + nc, 0:HALF] = o0.astype(BF16)
    out_ref[0, CMP_PAD:CMP_PAD + nc, HALF:KV_W] = o1.astype(BF16)


def _compress_prompt(ckv3, cw):
    w1, w2, pex, b1 = cw
    B, T, _ = ckv3.shape
    nc = T // CMP_STRIDE
    x = ckv3.reshape(B, nc, CMP_STRIDE * KV_W)
    return pl.pallas_call(
        _compress_body,
        grid=(B,),
        in_specs=[pl.BlockSpec((1, nc, CMP_STRIDE * KV_W), lambda b: (b, 0, 0)),
                  _full_spec(pex), _full_spec(w1), _full_spec(w2), _full_spec(b1)],
        out_specs=pl.BlockSpec((1, nc + 2 * CMP_PAD, KV_W), lambda b: (b, 0, 0)),
        out_shape=jax.ShapeDtypeStruct((B, nc + 2 * CMP_PAD, KV_W), BF16),
        compiler_params=pltpu.CompilerParams(dimension_semantics=("parallel",), vmem_limit_bytes=VMEM_LIMIT),
        name="compress_prompt",
    )(x, pex, w1, w2, b1)


def _stack_heads(q_ref, kv):
    return jnp.concatenate([q_ref[:, (kv * GROUP + g) * LANES:(kv * GROUP + g + 1) * LANES] for g in range(GROUP)],
                           axis=0)


def _softmax_rows(s):
    m = jnp.max(s, axis=-1, keepdims=True)
    p = jnp.exp(s - jnp.where(m == NEG_INF, 0.0, m))
    return p / jnp.maximum(jnp.sum(p, axis=-1, keepdims=True), 1e-30)


def _softmax_pair(a, b):
    m = jnp.maximum(jnp.max(a, axis=-1, keepdims=True), jnp.max(b, axis=-1, keepdims=True))
    m = jnp.where(m == NEG_INF, 0.0, m)
    pa = jnp.exp(a - m)
    pb = jnp.exp(b - m)
    l = jnp.maximum(jnp.sum(pa, axis=-1, keepdims=True) + jnp.sum(pb, axis=-1, keepdims=True), 1e-30)
    return pa / l, pb / l


def _select_topk(val, k):
    col = lax.broadcasted_iota(I32, val.shape, 1).astype(F32)
    sel = jnp.zeros(val.shape, F32)
    for _ in range(k):
        mx = jnp.max(val, axis=-1, keepdims=True)
        first = jnp.min(jnp.where(val == mx, col, float(val.shape[1])), axis=-1, keepdims=True)
        hit = col == first
        sel = jnp.where(hit, 1.0, sel)
        val = jnp.where(hit, NEG_INF, val)
    return sel


def _online_update(carry, s, v2):
    m, l, acc = carry
    m_new = jnp.maximum(m, jnp.max(s, axis=-1, keepdims=True))
    m_safe = jnp.where(m_new == NEG_INF, 0.0, m_new)
    alpha = jnp.exp(m - m_safe)
    p = jnp.exp(s - m_safe)
    l = alpha * l + jnp.sum(p, axis=-1, keepdims=True)
    acc = alpha * acc + _dot(p.astype(BF16), v2)
    return m_new, l, acc


def _online_init(rows):
    return (jnp.full((rows, 1), NEG_INF, F32), jnp.zeros((rows, 1), F32), jnp.zeros((rows, LANES), F32))


def _online_finish(carry):
    _, l, acc = carry
    return acc / jnp.maximum(l, 1e-30)


def _add_near_bias(s, kv, m0_ref, m1_ref):
    n_far = s.shape[1] - 2 * Q_BLOCK
    near = jnp.concatenate(
        [jnp.concatenate([m1_ref[kv * GROUP + g], m0_ref[kv * GROUP + g]], axis=1) for g in range(GROUP)], axis=0)
    return jnp.concatenate([s[:, :n_far], s[:, n_far:] + near], axis=1)


def _mask_rows(keep, s):
    n = s.shape[1]
    return jnp.where(keep[None], s.reshape(GROUP, Q_BLOCK, n), NEG_INF).reshape(GROUP * Q_BLOCK, n)


def _nsa_body(q_ref, misc_ref, ccmp_ref, skv_ref, wkv_ref, ovl_ref, tc_ref, m0_ref, m1_ref, o_ref):
    qb = pl.program_id(1)
    t0 = qb * Q_BLOCK
    cw = ccmp_ref.shape[1]
    n_tiles = qb // (KEY_TILE // Q_BLOCK) + 1
    gates = jax.nn.sigmoid(misc_ref[:, _NG_COL:_NG_COL + 3 * NSA_HEADS])

    row_c = lax.broadcasted_iota(I32, (Q_BLOCK, cw), 0)
    col_c = lax.broadcasted_iota(I32, (Q_BLOCK, cw), 1)
    n_c = col_c - CMP_PAD
    cmp_valid = (n_c >= 0) & (CMP_STRIDE * n_c + (CMP_BLOCK - 1) <= t0 + row_c)

    row_b = lax.broadcasted_iota(I32, (Q_BLOCK, LANES), 0)
    blk = lax.broadcasted_iota(I32, (Q_BLOCK, LANES), 1)
    cur = (t0 + row_b) // SEL_BLOCK
    forced = (blk == 0) | (blk == cur) | (blk == cur - 1)

    row_t = lax.broadcasted_iota(I32, (Q_BLOCK, KEY_TILE), 0)
    col_t = lax.broadcasted_iota(I32, (Q_BLOCK, KEY_TILE), 1)
    e_row = lax.broadcasted_iota(I32, (LANES, KEY_TILE), 0)
    e_col = lax.broadcasted_iota(I32, (LANES, KEY_TILE), 1) // SEL_BLOCK

    row_w = lax.broadcasted_iota(I32, (Q_BLOCK, WINDOW + Q_BLOCK), 0)
    col_w = lax.broadcasted_iota(I32, (Q_BLOCK, WINDOW + Q_BLOCK), 1)
    win_valid = (col_w <= WINDOW + row_w) & (col_w > row_w) & (col_w >= WINDOW - t0)

    kvs = range(NSA_KV_HEADS)
    qs = [_stack_heads(q_ref, kv) for kv in kvs]

    o_cmp, imps = [], []
    cv2 = ccmp_ref[0, :, HALF:KV_W]
    for kv in kvs:
        s = _dot_nt(qs[kv], ccmp_ref[0, :, 0:HALF])
        imp = jnp.zeros((Q_BLOCK, LANES), F32)
        for g in range(GROUP):
            table = jnp.concatenate([tc_ref[kv * GROUP + g], jnp.zeros((Q_BLOCK, cw - CMP_NEAR), F32)], axis=1)
            bias = pltpu.roll(table, (Q_BLOCK // CMP_STRIDE) * qb, axis=1)
            sg = jnp.where(cmp_valid, s[g * Q_BLOCK:(g + 1) * Q_BLOCK] + bias, NEG_INF)
            pb = _softmax_rows(sg).astype(BF16)
            o_cmp.append(_dot(pb, cv2))
            imp = imp + _dot(pb, ovl_ref[...])
        imps.append(jnp.where(forced, jnp.inf, jnp.where(blk <= cur, imp, NEG_INF)))

    sel = _select_topk(jnp.concatenate(imps, axis=0), SEL_TOPK)
    sel_b = [jnp.where(blk <= cur, sel[kv * Q_BLOCK:(kv + 1) * Q_BLOCK], 0.0).astype(BF16) for kv in kvs]

    o_win = []
    w_rows = pl.ds(pl.multiple_of(t0, Q_BLOCK), WINDOW + Q_BLOCK)
    for kv in kvs:
        sw = _add_near_bias(_dot_nt(qs[kv], wkv_ref[0, w_rows, 0:HALF]), kv, m0_ref, m1_ref)
        pw = _softmax_rows(_mask_rows(win_valid, sw)).astype(BF16)
        o_win.append(_dot(pw, wkv_ref[0, w_rows, HALF:KV_W]))

    def sel_tile(m, carry, near):
        row0 = pl.multiple_of(Q_BLOCK * (qb - (KEY_TILE // Q_BLOCK) * m), Q_BLOCK)
        blk0 = (row0 - KEY_PAD) // SEL_BLOCK
        expand = jnp.where(e_row == blk0 + e_col, 1.0, 0.0).astype(BF16)
        out = []
        for kv in kvs:
            tok = _dot(sel_b[kv], expand) > 0.5
            sc = _dot_nt(qs[kv], skv_ref[0, pl.ds(row0, KEY_TILE), 0:HALF])
            if near:
                tok = tok & (col_t - KEY_PAD <= row_t)
                sc = _add_near_bias(sc, kv, m0_ref, m1_ref)
            out.append(_online_update(carry[kv], _mask_rows(tok, sc), skv_ref[0, pl.ds(row0, KEY_TILE), HALF:KV_W]))
        return tuple(out)

    carry = sel_tile(0, tuple(_online_init(GROUP * Q_BLOCK) for _ in kvs), True)
    carry = lax.fori_loop(1, n_tiles, lambda m, c: sel_tile(m, c, False), carry)

    for kv in kvs:
        o_sel = _online_finish(carry[kv])
        lane_keep = _lane_keep(kv, Q_BLOCK)
        for g in range(GROUP):
            a = kv * GROUP + g
            rows = slice(g * Q_BLOCK, (g + 1) * Q_BLOCK)
            o = (gates[:, 3 * a:3 * a + 1] * o_cmp[a] + gates[:, 3 * a + 1:3 * a + 2] * o_sel[rows]
                 + gates[:, 3 * a + 2:3 * a + 3] * o_win[kv][rows])
            o_ref[:, a * LANES:(a + 1) * LANES] = (o * lane_keep).astype(BF16)


def _overlap_padded(n_chunks, cw):
    n = np.arange(cw)[:, None] - CMP_PAD
    j = np.arange(LANES)[None, :]
    n_cmp = n_chunks - CMP_RATIO + 1
    m = (n >= 0) & (n < n_cmp) & (n * CMP_STRIDE < (j + 1) * SEL_BLOCK) & (n * CMP_STRIDE + CMP_BLOCK > j * SEL_BLOCK)
    return jnp.asarray(m.astype(np.float32), BF16)


def _nsa_prompt(nq, misc, ccmp, skvb, wkvb, tables, B, T):
    m0, m1, tc = tables
    n_qb = T // Q_BLOCK
    assert T % KEY_TILE == 0 and SEL_TOPK <= T // SEL_BLOCK <= LANES
    cw = ccmp.shape[1]
    ovl = _overlap_padded(T // CMP_STRIDE, cw)
    skv_p = jnp.pad(skvb.reshape(B, T, KV_W), ((0, 0), (KEY_PAD, 0), (0, 0)))
    wkv_p = jnp.pad(wkvb.reshape(B, T, KV_W), ((0, 0), (WINDOW, 0), (0, 0)))
    rowblk = lambda w: pl.BlockSpec((Q_BLOCK, w), lambda b, i: (b * n_qb + i, 0))
    seq = lambda a: pl.BlockSpec((1,) + a.shape[1:], lambda b, i: (b, 0, 0))
    return pl.pallas_call(
        _nsa_body,
        grid=(B, n_qb),
        in_specs=[rowblk(Q_EXT), rowblk(_MISC_W), seq(ccmp), seq(skv_p), seq(wkv_p),
                  _full_spec(ovl), _full_spec(tc), _full_spec(m0), _full_spec(m1)],
        out_specs=rowblk(Q_EXT),
        out_shape=jax.ShapeDtypeStruct((B * T, Q_EXT), BF16),
        compiler_params=pltpu.CompilerParams(dimension_semantics=("parallel", "arbitrary"),
                                             vmem_limit_bytes=VMEM_LIMIT),
        name="nsa_prompt",
    )(nq, misc, ccmp, skv_p, wkv_p, ovl, tc, m0, m1)


_ORDER_OF_NEG_INF = -2 ** 31 + 0x7FFFFF


def _ordered_float(k):
    return lax.bitcast_convert_type(k ^ ((k >> 31) & 0x7FFFFFFF), F32)


def _kth_largest(count_ge, k, rows):
    zero = jnp.zeros((rows, 1), I32)
    ans = jnp.where(count_ge(_ordered_float(zero)) >= k, zero, jnp.full((rows, 1), INT_MIN, I32))

    def step(i, ans):
        cand = ans | jnp.left_shift(jnp.int32(1), 30 - i)
        return jnp.where(count_ge(_ordered_float(cand)) >= k, cand, ans)

    ans = lax.fori_loop(0, 31, step, ans)
    return jnp.where(ans > _ORDER_OF_NEG_INF, _ordered_float(ans), NEG_INF)


def _tie_cut(count_eq_below, r, n_bits, rows):
    def step(i, c):
        cand = c | jnp.left_shift(jnp.int32(1), n_bits - 1 - i)
        return jnp.where(count_eq_below(cand) < r, cand, c)

    return lax.fori_loop(0, n_bits, step, jnp.zeros((rows, 1), I32))


def _kept(s, col, floor, cut):
    return (s > floor) | ((s == floor) & (col <= cut))


def _topk_floor(count, min_score, last_col, k, n_bits, rows):
    kf = float(k)
    thr = _kth_largest(lambda cand: count(lambda s, col: s >= cand), kf, rows)
    finite = thr > NEG_INF
    n_gt = count(lambda s, col: s > thr)
    n_ge = count(lambda s, col: s >= thr)
    surplus = jnp.where(finite, n_ge - kf, 0.0)
    no_cut = jnp.full((rows, 1), 2 ** 31 - 1, I32)
    some_left = lambda left: jnp.max(left) > 0.0

    def by_search():
        return thr, _tie_cut(lambda c: count(lambda s, col: (s == thr) & (col < c)), kf - n_gt, n_bits, rows)

    def by_tie_drop():
        def drop(c):
            cut, left = c
            last = last_col(lambda s, col: (s == thr) & (col <= cut))
            return jnp.where(left > 0.0, last.astype(I32) - 1, cut), left - 1.0

        return thr, lax.while_loop(lambda c: some_left(c[1]), drop, (no_cut, surplus))[0]

    def by_value_drop():
        def drop(c):
            floor, cut, left = c
            worst = min_score(lambda s, col: _kept(s, col, floor, cut))
            last = last_col(lambda s, col: _kept(s, col, floor, cut) & (s == worst))
            go = left > 0.0
            return jnp.where(go, worst, floor), jnp.where(go, last.astype(I32) - 1, cut), left - 1.0

        return lax.while_loop(lambda c: some_left(c[2]), drop, (thr, no_cut, surplus))[:2]

    all_ties = jnp.min(jnp.where(surplus > 0.0, (n_ge - n_gt) - surplus, 1.0)) >= 1.0
    floor, cut = lax.cond(all_ties,
                          lambda: lax.cond(jnp.max(surplus) > 6.0, by_search, by_tie_drop),
                          by_value_drop)
    return floor, jnp.where(finite, cut, -1)


def _dsa_body(k_top, q_ref, iq_ref, misc_ref, ik_ref, dkv_ref, m0_ref, m1_ref, o_ref, key_scr):
    qb = pl.program_id(1)
    t0 = qb * Q_BLOCK
    tiles_per = KEY_TILE // Q_BLOCK
    n_tiles = qb // tiles_per + 1
    n_bits = int(key_scr.shape[1]).bit_length()
    row_t = lax.broadcasted_iota(I32, (Q_BLOCK, KEY_TILE), 0)
    col_t = lax.broadcasted_iota(I32, (Q_BLOCK, KEY_TILE), 1)

    def tile_row0(m):
        return pl.multiple_of(Q_BLOCK * (qb - tiles_per * m), Q_BLOCK)

    iq = jnp.concatenate([iq_ref[:, j * LANES:(j + 1) * LANES] for j in range(IDX_HEADS)], axis=0)
    iw = (misc_ref[:, _IW_COL:_IW_COL + IDX_HEADS] * IDX_HEADS ** -0.5).astype(BF16).astype(F32)

    def score_tile(m, _):
        row0 = tile_row0(m)
        x = _dot_nt(iq, ik_ref[0, pl.ds(row0, KEY_TILE), :])
        x = jnp.maximum(x, 0.0).astype(BF16).astype(F32)
        sc = x[0:Q_BLOCK] * iw[:, 0:1]
        for j in range(1, IDX_HEADS):
            sc = sc + x[j * Q_BLOCK:(j + 1) * Q_BLOCK] * iw[:, j:j + 1]
        pos = row0 - KEY_PAD + col_t
        valid = (pos >= 0) & (pos <= t0 + row_t)
        key_scr[:, pl.ds(row0, KEY_TILE)] = jnp.where(valid, jnp.where(sc == 0.0, 0.0, sc), NEG_INF)
        return 0

    lax.fori_loop(0, n_tiles, score_tile, 0)

    def fold_tiles(fn, init, combine):
        def body(m, acc):
            x = fn(key_scr[:, pl.ds(tile_row0(m), KEY_TILE)], tile_row0(m) + col_t)
            return combine(acc, combine(combine(x[:, 0:LANES], x[:, LANES:2 * LANES]),
                                        combine(x[:, 2 * LANES:3 * LANES], x[:, 3 * LANES:])))

        return lax.fori_loop(0, n_tiles, body, jnp.full((Q_BLOCK, LANES), init, F32))

    def count(pred):
        cnt = fold_tiles(lambda s, col: jnp.where(pred(s, col), 1.0, 0.0), 0.0, jnp.add)
        return jnp.sum(cnt, axis=-1, keepdims=True)

    def last_col(pred):
        last = fold_tiles(lambda s, col: jnp.where(pred(s, col), col.astype(F32), -1.0), -1.0, jnp.maximum)
        return jnp.max(last, axis=-1, keepdims=True)

    def min_score(pred):
        least = fold_tiles(lambda s, col: jnp.where(pred(s, col), s, jnp.inf), jnp.inf, jnp.minimum)
        return jnp.min(least, axis=-1, keepdims=True)

    floor, cut = _topk_floor(count, min_score, last_col, k_top, n_bits, Q_BLOCK)

    qs = [_stack_heads(q_ref, kv) for kv in range(DSA_KV_HEADS)]

    def attn_tile(m, carry, near):
        row0 = tile_row0(m)
        keep = _kept(key_scr[:, pl.ds(row0, KEY_TILE)], row0 + col_t, floor, cut)
        out = []
        for kv in range(DSA_KV_HEADS):
            sc = _dot_nt(qs[kv], dkv_ref[0, pl.ds(row0, KEY_TILE), 0:HALF])
            if near:
                sc = _add_near_bias(sc, kv, m0_ref, m1_ref)
            sc = _mask_rows(keep, sc)
            out.append(_online_update(carry[kv], sc, dkv_ref[0, pl.ds(row0, KEY_TILE), HALF:KV_W]))
        return tuple(out)

    carry = attn_tile(0, tuple(_online_init(GROUP * Q_BLOCK) for _ in range(DSA_KV_HEADS)), True)
    carry = lax.fori_loop(1, n_tiles, lambda m, c: attn_tile(m, c, False), carry)
    for kv in range(DSA_KV_HEADS):
        o = _online_finish(carry[kv])
        lane_keep = (lax.broadcasted_iota(I32, (Q_BLOCK, LANES), 1) // HEAD_DIM == kv).astype(F32)
        for g in range(GROUP):
            a = kv * GROUP + g
            o_ref[:, a * LANES:(a + 1) * LANES] = (o[g * Q_BLOCK:(g + 1) * Q_BLOCK] * lane_keep).astype(BF16)


def _dsa_prompt(dq, iq, misc, ikx, dkvb, tables, B, T):
    m0, m1, _ = tables
    n_qb = T // Q_BLOCK
    k_top = min(DSA_TOPK_MAX, T // 4)
    ik_p = jnp.pad(ikx.reshape(B, T, LANES), ((0, 0), (KEY_PAD, 0), (0, 0)))
    dkv_p = jnp.pad(dkvb.reshape(B, T, KV_W), ((0, 0), (KEY_PAD, 0), (0, 0)))
    rowblk = lambda w: pl.BlockSpec((Q_BLOCK, w), lambda b, i: (b * n_qb + i, 0))
    seq = lambda a: pl.BlockSpec((1,) + a.shape[1:], lambda b, i: (b, 0, 0))
    return pl.pallas_call(
        functools.partial(_dsa_body, k_top),
        grid=(B, n_qb),
        in_specs=[rowblk(Q_EXT), rowblk(IDX_HEADS * LANES), rowblk(_MISC_W), seq(ik_p), seq(dkv_p),
                  _full_spec(m0), _full_spec(m1)],
        out_specs=rowblk(Q_EXT),
        out_shape=jax.ShapeDtypeStruct((B * T, Q_EXT), BF16),
        scratch_shapes=[pltpu.VMEM((Q_BLOCK, T + KEY_PAD), F32)],
        compiler_params=pltpu.CompilerParams(dimension_semantics=("parallel", "arbitrary"),
                                             vmem_limit_bytes=VMEM_LIMIT),
        name="dsa_prompt",
    )(dq, iq, misc, ik_p, dkv_p, m0, m1)


def _ext_rows(w):
    d = w.shape[1]
    w4 = w.reshape(N_HEADS, 1, HEAD_DIM, d)
    sel = jnp.asarray(np.eye(2, dtype=np.float32)[[a // GROUP for a in range(N_HEADS)]], w.dtype)
    return (w4 * sel[:, :, None, None]).reshape(Q_EXT, d)


def _merge_body(x_ref, on_ref, od_ref, g_ref, wmg_ref, wa_ref, wb_ref, wo_ref, out_ref):
    x = x_ref[...]
    h = _rms(x, g_ref[...]).astype(BF16)
    m = jax.nn.sigmoid(_dot(h, wmg_ref[:, :D_MODEL])) * _dot(on_ref[...], wa_ref[...])
    m = m + jax.nn.sigmoid(_dot(h, wmg_ref[:, D_MODEL:])) * _dot(od_ref[...], wb_ref[...])
    out_ref[...] = x + _dot(m.astype(BF16), wo_ref[...])


def _merge(x2d, o_nsa, o_dsa, g, w_mg, w_a, w_b, w_o):
    rows = x2d.shape[0]
    tm = min(ROW_TILE, rows)
    row = lambda w: pl.BlockSpec((tm, w), lambda i: (i, 0))
    return pl.pallas_call(
        _merge_body,
        grid=(rows // tm,),
        in_specs=[row(D_MODEL), row(Q_EXT), row(Q_EXT), _full_spec(g), _full_spec(w_mg), _full_spec(w_a),
                  _full_spec(w_b), _full_spec(w_o)],
        out_specs=row(D_MODEL),
        out_shape=jax.ShapeDtypeStruct((rows, D_MODEL), F32),
        compiler_params=pltpu.CompilerParams(dimension_semantics=("parallel",), vmem_limit_bytes=VMEM_LIMIT),
        name="merge",
    )(x2d, o_nsa, o_dsa, g, w_mg, w_a, w_b, w_o)


def _ffn_body(n_chunks, x_ref, g_ref, gf_ref, wg_ref, wu_ref, wd_ref, out_ref):
    x = x_ref[...]
    h = _rms(x, g_ref[...]).astype(BF16)
    fc = wg_ref.shape[1] // n_chunks
    acc = x
    for c in range(n_chunks):
        gate = _dot(h, wg_ref[:, c * fc:(c + 1) * fc])
        up = _dot(h, wu_ref[:, c * fc:(c + 1) * fc])
        act = (gate * jax.nn.sigmoid(gate) * up).astype(BF16)
        acc = acc + _dot(act, wd_ref[c * fc:(c + 1) * fc, :])
    out_ref[...] = _rms(acc, gf_ref[...])


def _ffn(x2d, g, g_final, w_gate, w_up, w_down):
    rows = x2d.shape[0]
    tm = min(ROW_TILE, rows)
    d_ff = w_gate.shape[1]
    n_chunks = 2 if d_ff % (2 * LANES) == 0 else 1
    row = lambda w: pl.BlockSpec((tm, w), lambda i: (i, 0))
    return pl.pallas_call(
        functools.partial(_ffn_body, n_chunks),
        grid=(rows // tm,),
        in_specs=[row(D_MODEL), _full_spec(g), _full_spec(g_final), _full_spec(w_gate), _full_spec(w_up),
                  _full_spec(w_down)],
        out_specs=row(D_MODEL),
        out_shape=jax.ShapeDtypeStruct((rows, D_MODEL), F32),
        compiler_params=pltpu.CompilerParams(dimension_semantics=("parallel",), vmem_limit_bytes=VMEM_LIMIT),
        name="ffn",
    )(x2d, g, g_final, w_gate, w_up, w_down)


S_ROWS = 16


def _page_copy(pool_ref, page, buf_ref, slot, p, sem_ref):
    rows, width = pool_ref.shape[1:]
    if buf_ref.shape[2] == width:
        dst = buf_ref.at[slot, pl.ds(p * rows, rows)]
    else:
        dst = buf_ref.at[slot, :, pl.ds(p * width, width)]
    return pltpu.make_async_copy(pool_ref.at[page], dst, sem_ref.at[slot])


def _paged_step(pt_ref, pools):
    b = pl.program_id(0)
    n_pages = pt_ref.shape[1]
    slot = b % 2

    def fetch(seq, dst):
        for pool_ref, buf_ref, sem_ref in pools:
            for p in range(n_pages):
                _page_copy(pool_ref, pt_ref[seq, p], buf_ref, dst, p, sem_ref).start()

    @pl.when(b == 0)
    def _():
        fetch(0, 0)

    @pl.when(b + 1 < pl.num_programs(0))
    def _():
        fetch(b + 1, 1 - slot)

    for pool_ref, buf_ref, sem_ref in pools:
        for p in range(n_pages):
            _page_copy(pool_ref, 0, buf_ref, slot, p, sem_ref).wait()
    return slot


def _sum_over_heads(x):
    x = x + pltpu.roll(x, S_ROWS // 2, axis=0)
    return x + pltpu.roll(x, S_ROWS // 4, axis=0)


def _lane_keep(kv, rows):
    return (lax.broadcasted_iota(I32, (rows, LANES), 1) // HEAD_DIM == kv).astype(F32)


def _sample_cmp_body(pt_ref, pool_ref, q_ref, pex_ref, w1_ref, w2_ref, b1_ref, bias_ref, ovl_ref, selval_ref,
                     ocmp_ref, sel_ref, buf, sem, rows_scr):
    slot = _paged_step(pt_ref, [(pool_ref, buf, sem)])
    n_tok = rows_scr.shape[1]
    page = pool_ref.shape[2]

    unroll = 4

    def to_rows(i, _):
        for u in range(unroll):
            tok = pl.ds(pl.multiple_of((i * unroll + u) * page, page), page)
            for c in range(2):
                rows_scr[c, tok, :] = buf[slot, c * HALF:(c + 1) * HALF, tok].T
        return 0

    lax.fori_loop(0, n_tok // (page * unroll), to_rows, 0)
    nc = n_tok // CMP_STRIDE
    chunk_cols = lambda l, c: rows_scr[c, pl.ds(l, nc, stride=CMP_STRIDE), :]
    o0, o1 = _compress_chunks(chunk_cols, nc, pex_ref[...], w1_ref, w2_ref, b1_ref)
    ck2 = o0.astype(BF16)
    cv2 = o1.astype(BF16)
    sv = selval_ref[...]
    for kv in range(NSA_KV_HEADS):
        pb = _softmax_rows(_dot_nt(q_ref[0, kv], ck2) + bias_ref[kv]).astype(BF16)
        ocmp_ref[0, kv] = _dot(pb, cv2)
        imp = _sum_over_heads(_dot(pb, ovl_ref[...]))
        sel = _select_topk(jnp.where(sv == 0.0, imp, sv), SEL_TOPK)
        sel_ref[0, kv] = jnp.where(sv == NEG_INF, 0.0, sel)


def _sample_sel_body(pt_ref, pool_ref, q_ref, selm_ref, ocmp_ref, ng_ref, snew_ref, swin_ref, wnew_ref,
                     sbias_ref, snbias_ref, wbias_ref, wnbias_ref, e_ref, enew_ref, o_ref, buf, sem):
    slot = _paged_step(pt_ref, [(pool_ref, buf, sem)])
    kt = buf[slot, 0:HALF, :].astype(BF16)
    vt = buf[slot, HALF:KV_W, :].astype(BF16)
    kn = snew_ref[0, :, 0:HALF].astype(BF16)
    vn = snew_ref[0, :, HALF:KV_W].astype(BF16)
    kwt = swin_ref[0, 0:HALF, :].astype(BF16)
    vwt = swin_ref[0, HALF:KV_W, :].astype(BF16)
    kwn = wnew_ref[0, :, 0:HALF].astype(BF16)
    vwn = wnew_ref[0, :, HALF:KV_W].astype(BF16)
    for kv in range(NSA_KV_HEADS):
        q = q_ref[0, kv]
        selm = selm_ref[0, kv].astype(BF16)
        tok = _dot(selm, e_ref[...]) > 0.5
        tokn = _dot(selm, enew_ref[...]) > 0.5
        p, pn = _softmax_pair(jnp.where(tok, _dot(q, kt) + sbias_ref[kv], NEG_INF),
                              jnp.where(tokn, _dot_nt(q, kn) + snbias_ref[kv], NEG_INF))
        o_sel = _dot_nt(p.astype(BF16), vt) + _dot(pn.astype(BF16), vn)
        pw, pwn = _softmax_pair(_dot(q, kwt) + wbias_ref[kv], _dot_nt(q, kwn) + wnbias_ref[kv])
        o_win = _dot_nt(pw.astype(BF16), vwt) + _dot(pwn.astype(BF16), vwn)
        g = jax.nn.sigmoid(ng_ref[0, kv])
        o = g[:, 0:1] * ocmp_ref[0, kv] + g[:, 1:2] * o_sel + g[:, 2:3] * o_win
        o_ref[0, kv] = (o * _lane_keep(kv, S_ROWS)).astype(BF16)


def _sample_dsa_body(k_top, pt_ref, ipool_ref, kpool_ref, iq_ref, iw_ref, q_ref, inew_ref, knew_ref,
                     dbias_ref, dnbias_ref, newvalid_ref, o_ref, ibuf, kbuf, isem, ksem):
    slot = _paged_step(pt_ref, [(ipool_ref, ibuf, isem), (kpool_ref, kbuf, ksem)])
    n_keys = ibuf.shape[2]
    iq = iq_ref[0]
    iw = iw_ref[0].astype(BF16).astype(F32)

    def index_score(x):
        x = jnp.maximum(x, 0.0).astype(BF16).astype(F32)
        sc = _sum_over_heads(x * iw)
        return jnp.where(sc == 0.0, 0.0, sc)

    sc_c = index_score(_dot(iq, ibuf[slot].astype(BF16)))
    sc_n = jnp.where(newvalid_ref[...] == 0.0, index_score(_dot_nt(iq, inew_ref[0].astype(BF16))), NEG_INF)
    half = S_ROWS // 2
    hc, hn = sc_c[0:half], sc_n[0:half]
    col_c = lax.broadcasted_iota(I32, hc.shape, 1)
    col_n = n_keys + lax.broadcasted_iota(I32, hn.shape, 1)

    def reducer(where_fn, reduce_fn, combine):
        def run(pred):
            one = lambda s, col: reduce_fn(where_fn(pred(s, col), s, col), axis=-1, keepdims=True)
            return combine(one(hc, col_c), one(hn, col_n))
        return run

    count = reducer(lambda hit, s, col: jnp.where(hit, 1.0, 0.0), jnp.sum, jnp.add)
    min_score = reducer(lambda hit, s, col: jnp.where(hit, s, jnp.inf), jnp.min, jnp.minimum)
    last_col = reducer(lambda hit, s, col: jnp.where(hit, col.astype(F32), -1.0), jnp.max, jnp.maximum)
    floor, cut = _topk_floor(count, min_score, last_col, k_top, int(n_keys + S_ROWS).bit_length(), half)
    floor, cut = (jnp.concatenate([x, x], axis=0) for x in (floor, cut))
    keep_c = _kept(sc_c, lax.broadcasted_iota(I32, sc_c.shape, 1), floor, cut)
    keep_n = _kept(sc_n, n_keys + lax.broadcasted_iota(I32, sc_n.shape, 1), floor, cut)

    kt = kbuf[slot, 0:HALF, :].astype(BF16)
    vt = kbuf[slot, HALF:KV_W, :].astype(BF16)
    kn = knew_ref[0, :, 0:HALF].astype(BF16)
    vn = knew_ref[0, :, HALF:KV_W].astype(BF16)
    for kv in range(DSA_KV_HEADS):
        q = q_ref[0, kv]
        p, pn = _softmax_pair(jnp.where(keep_c, _dot(q, kt) + dbias_ref[kv], NEG_INF),
                              jnp.where(keep_n, _dot_nt(q, kn) + dnbias_ref[kv], NEG_INF))
        o = _dot_nt(p.astype(BF16), vt) + _dot(pn.astype(BF16), vn)
        o_ref[0, kv] = (o * _lane_keep(kv, S_ROWS)).astype(BF16)


def _sample_tables(tab_nsa, tab_dsa, past, dt, w_buf):
    n_chunks = past // CMP_STRIDE
    n_cmp = n_chunks - CMP_RATIO + 1
    qpos = past + (np.arange(S_ROWS) % dt)
    t_of = (np.arange(S_ROWS) % dt)[:, None]

    def per_head(tab, dist, valid):
        idx = _np_bucket(dist)
        far_col = np.all((idx == N_BUCKETS - 1) | ~valid, axis=0)
        n_far = int(np.argmin(far_col)) if not far_col.all() else idx.shape[1]
        out = []
        for kv in range(2):
            rows = []
            for g in range(GROUP):
                a = kv * GROUP + g
                near = tab[idx[g * dt:(g + 1) * dt, n_far:], a]
                rows.append(jnp.concatenate([jnp.broadcast_to(tab[N_BUCKETS - 1, a], (dt, n_far)), near], axis=1))
            out.append(jnp.concatenate(rows, axis=0))
        return jnp.where(jnp.asarray(valid)[None], jnp.stack(out), NEG_INF)

    n = np.arange(n_chunks)[None, :]
    d = qpos[:, None] - (CMP_STRIDE * n + CMP_BLOCK - 1)
    cmp_bias = per_head(tab_nsa, d, (d >= 0) & (n < n_cmp))
    s = np.arange(past)[None, :]
    d = qpos[:, None] - s
    sel_bias = per_head(tab_nsa, d, d >= 0)
    dsa_bias = per_head(tab_dsa, d, d >= 0)
    u = np.arange(S_ROWS)[None, :]
    new_ok = (u <= t_of) & (u < dt)
    new_bias_nsa = per_head(tab_nsa, t_of - u, new_ok)
    new_bias_dsa = per_head(tab_dsa, t_of - u, new_ok)
    i = np.arange(w_buf)[None, :]
    d = qpos[:, None] - (past - w_buf + i)
    win_bias = per_head(tab_nsa, d, (d >= 0) & (d < WINDOW) & (past - w_buf + i >= 0))
    new_valid = np.where(new_ok, 0.0, np.where(u < dt, -np.inf, 1.0)).astype(np.float32)

    n_sel = -(-(past + dt) // SEL_BLOCK)
    sel_w = -(-n_sel // LANES) * LANES
    j = np.arange(sel_w)[None, :]
    cur = (qpos // SEL_BLOCK)[:, None]
    forced = (j == 0) | (j == cur) | (j == cur - 1)
    sel_val = np.where(forced, np.inf, np.where((j <= cur) & (j < n_sel), 0.0, -np.inf)).astype(np.float32)
    nn = np.arange(n_chunks)[:, None]
    ovl = ((nn < n_cmp) & (nn * CMP_STRIDE < (j + 1) * SEL_BLOCK) & (nn * CMP_STRIDE + CMP_BLOCK > j * SEL_BLOCK))
    jj = np.arange(sel_w)[:, None]
    expand = jj == (np.arange(past)[None, :] // SEL_BLOCK)
    expand_new = (jj == ((past + u) // SEL_BLOCK)) & (u < dt)
    b16 = lambda m: jnp.asarray(m.astype(np.float32), BF16)
    return dict(cmp_bias=cmp_bias, sel_bias=sel_bias, dsa_bias=dsa_bias, new_bias_nsa=new_bias_nsa,
                new_bias_dsa=new_bias_dsa, win_bias=win_bias, new_valid=jnp.asarray(new_valid),
                sel_val=jnp.asarray(sel_val), ovl=b16(ovl), expand=b16(expand), expand_new=b16(expand_new))


def _sample_mixers(ps, DB, DT, c_cmp, c_sel, c_dkv, c_idx, s_win, page_table, cw, tab_nsa, tab_dsa):
    w1, w2, pex, b1 = cw
    n_pool, page = c_cmp.shape[:2]
    n_pages = page_table.shape[1]
    past = n_pages * page
    w_buf = s_win.shape[1]
    assert DT * GROUP == S_ROWS and past % KEY_TILE == 0
    tb = _sample_tables(tab_nsa, tab_dsa, past, DT, w_buf)
    k_top = min(DSA_TOPK_MAX, (past + DT) // 4)
    sel_w = tb["sel_val"].shape[1]

    rows = lambda a, w: jnp.transpose(a.reshape(DB, DT, 2, GROUP, w), (0, 2, 3, 1, 4)).reshape(DB, 2, S_ROWS, w)
    nq = rows(ps["nq"], LANES)
    dq = rows(ps["dq"], LANES)
    ng = rows(ps["misc"][:, _NG_COL:_NG_COL + 3 * NSA_HEADS], 3)
    iq = jnp.transpose(ps["iq"].reshape(DB, DT, IDX_HEADS, LANES)[..., :IDX_DIM], (0, 2, 1, 3)).reshape(
        DB, S_ROWS, IDX_DIM)
    iw = jnp.transpose(ps["misc"][:, _IW_COL:_IW_COL + IDX_HEADS].reshape(DB, DT, IDX_HEADS) * IDX_HEADS ** -0.5,
                       (0, 2, 1)).reshape(DB, S_ROWS, 1)
    new16 = lambda a: jnp.pad(a.reshape(DB, DT, a.shape[-1]), ((0, 0), (0, S_ROWS - DT), (0, 0)))
    s_new, w_new, d_new, i_new = new16(ps["skv"]), new16(ps["wkv"]), new16(ps["dkv"]), new16(ps["ik"])

    seq = lambda a: pl.BlockSpec((1,) + a.shape[1:], lambda b, pt: (b,) + (0,) * (a.ndim - 1))
    const = lambda a: pl.BlockSpec(a.shape, lambda b, pt: (0,) * a.ndim)
    hbm = pl.BlockSpec(memory_space=pl.ANY)
    params = pltpu.CompilerParams(dimension_semantics=("arbitrary",), vmem_limit_bytes=VMEM_LIMIT)
    grp = lambda w, dt: jax.ShapeDtypeStruct((DB, 2, S_ROWS, w), dt)
    out_grp = lambda w: pl.BlockSpec((1, 2, S_ROWS, w), lambda b, pt: (b, 0, 0, 0))

    token_minor = lambda a: jnp.moveaxis(a, 1, -1).reshape(a.shape[0], -1, a.shape[1])
    cmp_pool = token_minor(c_cmp)
    ins = [nq, pex, w1, w2, b1, tb["cmp_bias"], tb["ovl"], tb["sel_val"]]
    o_cmp, selm = pl.pallas_call(
        _sample_cmp_body,
        grid_spec=pltpu.PrefetchScalarGridSpec(
            num_scalar_prefetch=1, grid=(DB,),
            in_specs=[hbm, seq(nq)] + [const(a) for a in ins[1:]],
            out_specs=[out_grp(LANES), out_grp(sel_w)],
            scratch_shapes=[pltpu.VMEM((2, KV_W, past), F32), pltpu.SemaphoreType.DMA((2,)),
                            pltpu.VMEM((2, past, HALF), F32)]),
        out_shape=[grp(LANES, F32), grp(sel_w, F32)],
        compiler_params=params, name="sample_cmp",
    )(page_table, cmp_pool, *ins)

    sel_pool = token_minor(c_sel)
    s_win3 = token_minor(s_win)
    ins = [nq, selm, o_cmp, ng, s_new, s_win3, w_new]
    consts = [tb["sel_bias"], tb["new_bias_nsa"], tb["win_bias"], tb["new_bias_nsa"], tb["expand"], tb["expand_new"]]
    o_nsa = pl.pallas_call(
        _sample_sel_body,
        grid_spec=pltpu.PrefetchScalarGridSpec(
            num_scalar_prefetch=1, grid=(DB,),
            in_specs=[hbm] + [seq(a) for a in ins] + [const(a) for a in consts],
            out_specs=out_grp(LANES),
            scratch_shapes=[pltpu.VMEM((2, KV_W, past), F32), pltpu.SemaphoreType.DMA((2,))]),
        out_shape=grp(LANES, BF16),
        compiler_params=params, name="sample_sel",
    )(page_table, sel_pool, *ins, *consts)

    idx_pool = token_minor(c_idx)
    dkv_pool = token_minor(c_dkv)
    ins = [iq, iw, dq, i_new, d_new]
    consts = [tb["dsa_bias"], tb["new_bias_dsa"], tb["new_valid"]]
    o_dsa = pl.pallas_call(
        functools.partial(_sample_dsa_body, k_top),
        grid_spec=pltpu.PrefetchScalarGridSpec(
            num_scalar_prefetch=1, grid=(DB,),
            in_specs=[hbm, hbm] + [seq(a) for a in ins] + [const(a) for a in consts],
            out_specs=out_grp(LANES),
            scratch_shapes=[pltpu.VMEM((2, IDX_DIM, past), F32), pltpu.VMEM((2, KV_W, past), F32),
                            pltpu.SemaphoreType.DMA((2,)), pltpu.SemaphoreType.DMA((2,))]),
        out_shape=grp(LANES, BF16),
        compiler_params=params, name="sample_dsa",
    )(page_table, idx_pool, dkv_pool, *ins, *consts)

    unrows = lambda o: jnp.transpose(o.reshape(DB, 2, GROUP, DT, LANES), (0, 3, 1, 2, 4)).reshape(DB * DT, Q_EXT)
    return unrows(o_nsa), unrows(o_dsa)


def kernel(x_prompt, x_sample, cache_nsa_cmp, cache_nsa_sel, cache_dsa_kv, cache_dsa_idx, state_nsa_win, page_table, norm_mix, w_in, cmp_pe, cmp_w1, cmp_b1, cmp_w2, rel_bias, w_branch_nsa, w_branch_dsa, w_out, norm_ffn, w_gate, w_up, w_down, norm_final):
    assert norm_mix.shape[0] == 1, "single trunk layer"
    B, T, _ = x_prompt.shape
    DB, DT, _ = x_sample.shape
    tab_nsa = rel_bias[:, :NSA_HEADS]
    tab_dsa = rel_bias[:, NSA_HEADS:]
    l = 0
    g_mix = norm_mix[l][None]
    g_ffn = norm_ffn[l][None]
    g_fin = norm_final[None]
    w_pack, w_mg = _pack_w_in(w_in[l])
    w_a = _ext_rows(w_branch_nsa[l]).astype(BF16)
    w_b = _ext_rows(w_branch_dsa[l]).astype(BF16)
    w_o = w_out[l].astype(BF16)
    wg, wu, wd = (w.astype(BF16) for w in (w_gate[l], w_up[l], w_down[l]))
    cw = _pack_cmp_weights(cmp_pe[l], cmp_w1[l], cmp_b1[l], cmp_w2[l])

    def tail(x2d, o_n, o_d):
        x1 = _merge(x2d, o_n, o_d, g_mix, w_mg, w_a, w_b, w_o)
        return _ffn(x1, g_ffn, g_fin, wg, wu, wd)

    xp = x_prompt.reshape(B * T, D_MODEL)
    pp = _project(xp, g_mix, w_pack)
    ccmp = _compress_prompt(pp["ckv"].reshape(B, T, KV_W), cw)
    o_n = _nsa_prompt(pp["nq"], pp["misc"], ccmp, pp["skvb"], pp["wkvb"], _prompt_tables(tab_nsa), B, T)
    o_d = _dsa_prompt(pp["dq"], pp["iq"], pp["misc"], pp["ikx"], pp["dkvb"], _prompt_tables(tab_dsa), B, T)
    y_prompt = tail(xp, o_n, o_d).reshape(B, T, D_MODEL)

    xs = x_sample.reshape(DB * DT, D_MODEL)
    ps = _project(xs, g_mix, w_pack)
    o_n, o_d = _sample_mixers(ps, DB, DT, cache_nsa_cmp[l], cache_nsa_sel[l], cache_dsa_kv[l], cache_dsa_idx[l],
                              state_nsa_win[l], page_table, cw, tab_nsa, tab_dsa)
    y_sample = tail(xs, o_n, o_d).reshape(DB, DT, D_MODEL)

    kv6 = lambda a, b, t: a.reshape(1, b, t, 2, NSA_KV_HEADS, HEAD_DIM)
    win_s = jnp.concatenate([state_nsa_win[l], kv6(ps["wkv"], DB, DT)[0].astype(state_nsa_win.dtype)], axis=1)[:, DT:]
    w_keep = min(WINDOW, T)
    return (y_prompt, y_sample,
            kv6(pp["ckv"], B, T), kv6(ps["ckv"], DB, DT),
            kv6(pp["skv"], B, T), kv6(ps["skv"], DB, DT),
            kv6(pp["dkv"], B, T), kv6(ps["dkv"], DB, DT),
            pp["ik"].reshape(1, B, T, IDX_DIM), ps["ik"].reshape(1, DB, DT, IDX_DIM),
            kv6(pp["wkv"], B, T)[:, :, T - w_keep:], win_s[None])
```

```python
import functools
import math

import jax
import jax.numpy as jnp
import numpy as np
from jax import lax
from jax.experimental import pallas as pl
from jax.experimental.pallas import tpu as pltpu

F32 = jnp.float32
BF16 = jnp.bfloat16
I32 = jnp.int32

D_MODEL = 1024
HEAD_DIM = 64
NSA_HEADS = 8
NSA_KV_HEADS = 2
NSA_GROUP = NSA_HEADS // NSA_KV_HEADS
CMP_BLOCK = 32
CMP_STRIDE = 16
CMP_RATIO = CMP_BLOCK // CMP_STRIDE
CMP_HIDDEN = 2 * HEAD_DIM
SEL_BLOCK = 64
SEL_TOPK = 16
WINDOW = 512
DSA_HEADS = 8
DSA_KV_HEADS = 2
DSA_GROUP = DSA_HEADS // DSA_KV_HEADS
IDX_HEADS = 4
IDX_DIM = 64
DSA_TOPK_MAX = 256
N_BUCKETS = 32
MAX_DISTANCE = 128
Q_BLOCK = 128
EPS = 1e-6
SCALE = HEAD_DIM ** -0.5

LANES = 128
KV_W = 2 * NSA_KV_HEADS * HEAD_DIM
HALF = KV_W // 2
N_HEADS = NSA_HEADS
GROUP = NSA_GROUP
Q_EXT = N_HEADS * LANES
KEY_TILE = 512
KEY_PAD = KEY_TILE - Q_BLOCK
CMP_PAD = 128
CMP_NEAR = 256
PROJ_WIDTHS = (NSA_HEADS * HEAD_DIM, KV_W, KV_W, KV_W, 3 * NSA_HEADS, DSA_HEADS * HEAD_DIM, KV_W,
               IDX_HEADS * IDX_DIM, IDX_DIM, IDX_HEADS, 2 * D_MODEL)
PROJ_OFFS = tuple(int(v) for v in np.cumsum((0,) + PROJ_WIDTHS))
INT_MIN = -2 ** 31
NEG_INF = float("-inf")

VMEM_LIMIT = 56 * 1024 * 1024
ROW_TILE = 512


def _rms(x, g):
    return x * lax.rsqrt(jnp.mean(x * x, axis=-1, keepdims=True) + EPS) * g


def _dot(a, b):
    return jnp.dot(a, b, preferred_element_type=F32)


def _dot_nt(a, b):
    return lax.dot_general(a, b, (((1,), (1,)), ((), ())), preferred_element_type=F32)


def _full_spec(a):
    nd = a.ndim
    return pl.BlockSpec(a.shape, lambda *_: (0,) * nd)


_MISC_W = LANES
_IW_COL = 0
_NG_COL = IDX_HEADS
_PROJ_OUT = (
    ("nq", Q_EXT, BF16), ("dq", Q_EXT, BF16), ("iq", IDX_HEADS * LANES, BF16), ("ikx", LANES, BF16),
    ("ckv", KV_W, F32), ("skv", KV_W, F32), ("wkv", KV_W, F32), ("dkv", KV_W, F32), ("misc", _MISC_W, F32))


def _ext_cols(w, n_heads, kv_of_head):
    d = w.shape[0]
    w4 = w.reshape(d, n_heads, 1, HEAD_DIM)
    sel = jnp.asarray(np.eye(2, dtype=np.float32)[[kv_of_head(a) for a in range(n_heads)]], w.dtype)
    return (w4 * sel[None, :, :, None]).reshape(d, n_heads * LANES)


def _pack_w_in(w_in):
    o = PROJ_OFFS
    seg = lambda i: w_in[:, o[i]:o[i + 1]]
    zeros = lambda n: jnp.zeros((D_MODEL, n), w_in.dtype)
    cols = [_ext_cols(seg(0), N_HEADS, lambda a: a // GROUP),
            _ext_cols(seg(5), N_HEADS, lambda a: a // GROUP),
            _ext_cols(seg(7), IDX_HEADS, lambda a: 0),
            seg(8), zeros(LANES - IDX_DIM),
            seg(1), seg(2), seg(3), seg(6),
            seg(9), seg(4), zeros(_MISC_W - IDX_HEADS - 3 * NSA_HEADS)]
    return jnp.concatenate(cols, axis=1).astype(BF16), seg(10).astype(BF16)


def _proj_body(x_ref, g_ref, w_ref, nq_ref, dq_ref, iq_ref, ikx_ref, ckv_ref, skv_ref, wkv_ref, dkv_ref, misc_ref,
               ik_ref, skvb_ref, wkvb_ref, dkvb_ref):
    h = _rms(x_ref[...], g_ref[...]).astype(BF16)
    c = 0

    def mm(w):
        nonlocal c
        r = _dot(h, w_ref[:, c:c + w])
        c += w
        return r

    nq_ref[...] = (mm(Q_EXT) * SCALE).astype(BF16)
    dq_ref[...] = (mm(Q_EXT) * SCALE).astype(BF16)
    iq_ref[...] = mm(IDX_HEADS * LANES).astype(BF16)
    ik = mm(LANES)
    ikx_ref[...] = ik.astype(BF16)
    ik_ref[...] = ik[:, :IDX_DIM]
    ckv_ref[...] = mm(KV_W)
    skv = mm(KV_W)
    skv_ref[...] = skv
    skvb_ref[...] = skv.astype(BF16)
    wkv = mm(KV_W)
    wkv_ref[...] = wkv
    wkvb_ref[...] = wkv.astype(BF16)
    dkv = mm(KV_W)
    dkv_ref[...] = dkv
    dkvb_ref[...] = dkv.astype(BF16)
    misc_ref[...] = mm(_MISC_W)


def _project(x2d, g, w_pack):
    rows = x2d.shape[0]
    tm = min(ROW_TILE, rows)
    row = lambda w: pl.BlockSpec((tm, w), lambda i: (i, 0))
    outs = list(_PROJ_OUT) + [("ik", IDX_DIM, F32), ("skvb", KV_W, BF16), ("wkvb", KV_W, BF16), ("dkvb", KV_W, BF16)]
    res = pl.pallas_call(
        _proj_body,
        grid=(rows // tm,),
        in_specs=[row(D_MODEL), _full_spec(g), _full_spec(w_pack)],
        out_specs=[row(w) for _, w, _ in outs],
        out_shape=[jax.ShapeDtypeStruct((rows, w), d) for _, w, d in outs],
        compiler_params=pltpu.CompilerParams(dimension_semantics=("parallel",), vmem_limit_bytes=VMEM_LIMIT),
        name="proj",
    )(x2d, g, w_pack)
    return {name: r for (name, _, _), r in zip(outs, res)}


def _np_bucket(dist):
    d = np.maximum(np.asarray(dist, np.int64), 0)
    max_exact = N_BUCKETS // 2
    logd = np.log(np.maximum(d, 1).astype(np.float64) / max_exact) / math.log(MAX_DISTANCE / max_exact)
    large = np.minimum(max_exact + (logd * (N_BUCKETS - max_exact)).astype(np.int64), N_BUCKETS - 1)
    return np.where(d < max_exact, d, large).astype(np.int32)


_FAR_DIST = 129
assert int(_np_bucket(_FAR_DIST - 16).min()) == N_BUCKETS - 1


def _prompt_tables(tab):
    rel = tab - tab[N_BUCKETS - 1][None, :]

    def lookup(dist):
        onehot = jnp.asarray(_np_bucket(dist))[..., None] == jnp.arange(N_BUCKETS)
        return jnp.einsum('ijb,bh->hij', onehot.astype(F32), rel, precision=lax.Precision.HIGHEST)

    i = np.arange(Q_BLOCK)[:, None]
    j = np.arange(Q_BLOCK)[None, :]
    c = np.arange(CMP_NEAR)[None, :]
    return lookup(i - j), lookup(Q_BLOCK + i - j), lookup(i - CMP_STRIDE * (c - CMP_PAD) - (CMP_BLOCK - 1))


def _pack_cmp_weights(cmp_pe, cmp_w1, cmp_b1, cmp_w2):
    eye = jnp.eye(NSA_KV_HEADS, dtype=cmp_w1.dtype)
    w1r = cmp_w1.reshape(2, CMP_RATIO, CMP_STRIDE, HEAD_DIM, CMP_HIDDEN)
    w1 = jnp.einsum('crldf,hk->clhdrkf', w1r, eye).reshape(
        2, CMP_STRIDE * HALF, CMP_RATIO * NSA_KV_HEADS * CMP_HIDDEN)
    w2 = jnp.einsum('cfd,hk->chfkd', cmp_w2, eye).reshape(2, NSA_KV_HEADS * CMP_HIDDEN, HALF)
    pe = cmp_pe.reshape(2, CMP_RATIO, CMP_STRIDE, HEAD_DIM)
    pex = jnp.broadcast_to(jnp.transpose(pe, (1, 2, 0, 3))[:, :, :, None, :],
                           (CMP_RATIO, CMP_STRIDE, 2, NSA_KV_HEADS, HEAD_DIM)).reshape(CMP_RATIO, CMP_STRIDE * KV_W)
    pex = jnp.concatenate([pex, jnp.zeros((8 - CMP_RATIO, CMP_STRIDE * KV_W), pex.dtype)], axis=0)
    b1 = jnp.concatenate([cmp_b1] * NSA_KV_HEADS, axis=-1)[:, None, :]
    return w1.astype(BF16), w2.astype(BF16), pex, b1


def _compress_chunks(chunk_cols, nc, pex, w1_ref, w2_ref, b1_ref):
    hw = NSA_KV_HEADS * CMP_HIDDEN
    outs = []
    for c in range(2):
        cat = lambda parts: jnp.concatenate(parts, axis=1).astype(BF16)
        hr = _dot(cat([chunk_cols(l, c) for l in range(CMP_STRIDE)]), w1_ref[c])
        hp = _dot(cat([pex[:, l * KV_W + c * HALF:l * KV_W + (c + 1) * HALF] for l in range(CMP_STRIDE)]), w1_ref[c])
        bias = b1_ref[c] + hp[0:1, 0:hw] + hp[1:2, hw:2 * hw]
        hsum = hr[:, 0:hw] + pltpu.roll(hr[:, hw:2 * hw], nc - 1, axis=0) + bias
        outs.append(_dot(jax.nn.gelu(hsum, approximate=True).astype(BF16), w2_ref[c]))
    return outs


def _compress_body(x_ref, pex_ref, w1_ref, w2_ref, b1_ref, out_ref):
    nc = x_ref.shape[1]
    chunk_cols = lambda l, c: x_ref[0, :, l * KV_W + c * HALF:l * KV_W + (c + 1) * HALF]
    o0, o1 = _compress_chunks(chunk_cols, nc, pex_ref[...], w1_ref, w2_ref, b1_ref)
    out_ref[0, 0:CMP_PAD, :] = jnp.zeros((CMP_PAD, KV_W), BF16)
    out_ref[0, CMP_PAD + nc:, :] = jnp.zeros((CMP_PAD, KV_W), BF16)
    out_ref[0, CMP_PAD:CMP_PAD + nc, 0:HALF] = o0.astype(BF16)
    out_ref[0, CMP_PAD:CMP_PAD + nc, HALF:KV_W] = o1.astype(BF16)


def _compress_prompt(ckv3, cw):
    w1, w2, pex, b1 = cw
    B, T, _ = ckv3.shape
    nc = T // CMP_STRIDE
    x = ckv3.reshape(B, nc, CMP_STRIDE * KV_W)
    return pl.pallas_call(
        _compress_body,
        grid=(B,),
        in_specs=[pl.BlockSpec((1, nc, CMP_STRIDE * KV_W), lambda b: (b, 0, 0)),
                  _full_spec(pex), _full_spec(w1), _full_spec(w2), _full_spec(b1)],
        out_specs=pl.BlockSpec((1, nc + 2 * CMP_PAD, KV_W), lambda b: (b, 0, 0)),
        out_shape=jax.ShapeDtypeStruct((B, nc + 2 * CMP_PAD, KV_W), BF16),
        compiler_params=pltpu.CompilerParams(dimension_semantics=("parallel",), vmem_limit_bytes=VMEM_LIMIT),
        name="compress_prompt",
    )(x, pex, w1, w2, b1)


def _stack_heads(q_ref, kv):
    return jnp.concatenate([q_ref[:, (kv * GROUP + g) * LANES:(kv * GROUP + g + 1) * LANES] for g in range(GROUP)],
                           axis=0)


def _softmax_rows(s):
    m = jnp.max(s, axis=-1, keepdims=True)
    p = jnp.exp(s - jnp.where(m == NEG_INF, 0.0, m))
    return p / jnp.maximum(jnp.sum(p, axis=-1, keepdims=True), 1e-30)


def _softmax_pair(a, b):
    m = jnp.maximum(jnp.max(a, axis=-1, keepdims=True), jnp.max(b, axis=-1, keepdims=True))
    m = jnp.where(m == NEG_INF, 0.0, m)
    pa = jnp.exp(a - m)
    pb = jnp.exp(b - m)
    l = jnp.maximum(jnp.sum(pa, axis=-1, keepdims=True) + jnp.sum(pb, axis=-1, keepdims=True), 1e-30)
    return pa / l, pb / l


def _select_topk(val, k):
    col = lax.broadcasted_iota(I32, val.shape, 1).astype(F32)
    sel = jnp.zeros(val.shape, F32)
    for _ in range(k):
        mx = jnp.max(val, axis=-1, keepdims=True)
        first = jnp.min(jnp.where(val == mx, col, float(val.shape[1])), axis=-1, keepdims=True)
        hit = col == first
        sel = jnp.where(hit, 1.0, sel)
        val = jnp.where(hit, NEG_INF, val)
    return sel


def _online_update(carry, s, v2):
    m, l, acc = carry
    m_new = jnp.maximum(m, jnp.max(s, axis=-1, keepdims=True))
    m_safe = jnp.where(m_new == NEG_INF, 0.0, m_new)
    alpha = jnp.exp(m - m_safe)
    p = jnp.exp(s - m_safe)
    l = alpha * l + jnp.sum(p, axis=-1, keepdims=True)
    acc = alpha * acc + _dot(p.astype(BF16), v2)
    return m_new, l, acc


def _online_init(rows):
    return (jnp.full((rows, 1), NEG_INF, F32), jnp.zeros((rows, 1), F32), jnp.zeros((rows, LANES), F32))


def _online_finish(carry):
    _, l, acc = carry
    return acc / jnp.maximum(l, 1e-30)


def _add_near_bias(s, kv, m0_ref, m1_ref):
    n_far = s.shape[1] - 2 * Q_BLOCK
    near = jnp.concatenate(
        [jnp.concatenate([m1_ref[kv * GROUP + g], m0_ref[kv * GROUP + g]], axis=1) for g in range(GROUP)], axis=0)
    return jnp.concatenate([s[:, :n_far], s[:, n_far:] + near], axis=1)


def _mask_rows(keep, s):
    n = s.shape[1]
    return jnp.where(keep[None], s.reshape(GROUP, Q_BLOCK, n), NEG_INF).reshape(GROUP * Q_BLOCK, n)


def _nsa_body(q_ref, misc_ref, ccmp_ref, skv_ref, wkv_ref, ovl_ref, tc_ref, m0_ref, m1_ref, o_ref):
    qb = pl.program_id(1)
    t0 = qb * Q_BLOCK
    cw = ccmp_ref.shape[1]
    n_tiles = qb // (KEY_TILE // Q_BLOCK) + 1
    gates = jax.nn.sigmoid(misc_ref[:, _NG_COL:_NG_COL + 3 * NSA_HEADS])

    row_c = lax.broadcasted_iota(I32, (Q_BLOCK, cw), 0)
    col_c = lax.broadcasted_iota(I32, (Q_BLOCK, cw), 1)
    n_c = col_c - CMP_PAD
    cmp_valid = (n_c >= 0) & (CMP_STRIDE * n_c + (CMP_BLOCK - 1) <= t0 + row_c)

    row_b = lax.broadcasted_iota(I32, (Q_BLOCK, LANES), 0)
    blk = lax.broadcasted_iota(I32, (Q_BLOCK, LANES), 1)
    cur = (t0 + row_b) // SEL_BLOCK
    forced = (blk == 0) | (blk == cur) | (blk == cur - 1)

    row_t = lax.broadcasted_iota(I32, (Q_BLOCK, KEY_TILE), 0)
    col_t = lax.broadcasted_iota(I32, (Q_BLOCK, KEY_TILE), 1)
    e_row = lax.broadcasted_iota(I32, (LANES, KEY_TILE), 0)
    e_col = lax.broadcasted_iota(I32, (LANES, KEY_TILE), 1) // SEL_BLOCK

    row_w = lax.broadcasted_iota(I32, (Q_BLOCK, WINDOW + Q_BLOCK), 0)
    col_w = lax.broadcasted_iota(I32, (Q_BLOCK, WINDOW + Q_BLOCK), 1)
    win_valid = (col_w <= WINDOW + row_w) & (col_w > row_w) & (col_w >= WINDOW - t0)

    kvs = range(NSA_KV_HEADS)
    qs = [_stack_heads(q_ref, kv) for kv in kvs]

    o_cmp, imps = [], []
    cv2 = ccmp_ref[0, :, HALF:KV_W]
    for kv in kvs:
        s = _dot_nt(qs[kv], ccmp_ref[0, :, 0:HALF])
        imp = jnp.zeros((Q_BLOCK, LANES), F32)
        for g in range(GROUP):
            table = jnp.concatenate([tc_ref[kv * GROUP + g], jnp.zeros((Q_BLOCK, cw - CMP_NEAR), F32)], axis=1)
            bias = pltpu.roll(table, (Q_BLOCK // CMP_STRIDE) * qb, axis=1)
            sg = jnp.where(cmp_valid, s[g * Q_BLOCK:(g + 1) * Q_BLOCK] + bias, NEG_INF)
            pb = _softmax_rows(sg).astype(BF16)
            o_cmp.append(_dot(pb, cv2))
            imp = imp + _dot(pb, ovl_ref[...])
        imps.append(jnp.where(forced, jnp.inf, jnp.where(blk <= cur, imp, NEG_INF)))

    sel = _select_topk(jnp.concatenate(imps, axis=0), SEL_TOPK)
    sel_b = [jnp.where(blk <= cur, sel[kv * Q_BLOCK:(kv + 1) * Q_BLOCK], 0.0).astype(BF16) for kv in kvs]

    o_win = []
    w_rows = pl.ds(pl.multiple_of(t0, Q_BLOCK), WINDOW + Q_BLOCK)
    for kv in kvs:
        sw = _add_near_bias(_dot_nt(qs[kv], wkv_ref[0, w_rows, 0:HALF]), kv, m0_ref, m1_ref)
        pw = _softmax_rows(_mask_rows(win_valid, sw)).astype(BF16)
        o_win.append(_dot(pw, wkv_ref[0, w_rows, HALF:KV_W]))

    def sel_tile(m, carry, near):
        row0 = pl.multiple_of(Q_BLOCK * (qb - (KEY_TILE // Q_BLOCK) * m), Q_BLOCK)
        blk0 = (row0 - KEY_PAD) // SEL_BLOCK
        expand = jnp.where(e_row == blk0 + e_col, 1.0, 0.0).astype(BF16)
        out = []
        for kv in kvs:
            tok = _dot(sel_b[kv], expand) > 0.5
            sc = _dot_nt(qs[kv], skv_ref[0, pl.ds(row0, KEY_TILE), 0:HALF])
            if near:
                tok = tok & (col_t - KEY_PAD <= row_t)
                sc = _add_near_bias(sc, kv, m0_ref, m1_ref)
            out.append(_online_update(carry[kv], _mask_rows(tok, sc), skv_ref[0, pl.ds(row0, KEY_TILE), HALF:KV_W]))
        return tuple(out)

    carry = sel_tile(0, tuple(_online_init(GROUP * Q_BLOCK) for _ in kvs), True)
    carry = lax.fori_loop(1, n_tiles, lambda m, c: sel_tile(m, c, False), carry)

    for kv in kvs:
        o_sel = _online_finish(carry[kv])
        lane_keep = _lane_keep(kv, Q_BLOCK)
        for g in range(GROUP):
            a = kv * GROUP + g
            rows = slice(g * Q_BLOCK, (g + 1) * Q_BLOCK)
            o = (gates[:, 3 * a:3 * a + 1] * o_cmp[a] + gates[:, 3 * a + 1:3 * a + 2] * o_sel[rows]
                 + gates[:, 3 * a + 2:3 * a + 3] * o_win[kv][rows])
            o_ref[:, a * LANES:(a + 1) * LANES] = (o * lane_keep).astype(BF16)


def _overlap_padded(n_chunks, cw):
    n = np.arange(cw)[:, None] - CMP_PAD
    j = np.arange(LANES)[None, :]
    n_cmp = n_chunks - CMP_RATIO + 1
    m = (n >= 0) & (n < n_cmp) & (n * CMP_STRIDE < (j + 1) * SEL_BLOCK) & (n * CMP_STRIDE + CMP_BLOCK > j * SEL_BLOCK)
    return jnp.asarray(m.astype(np.float32), BF16)


def _nsa_prompt(nq, misc, ccmp, skvb, wkvb, tables, B, T):
    m0, m1, tc = tables
    n_qb = T // Q_BLOCK
    assert T % KEY_TILE == 0 and SEL_TOPK <= T // SEL_BLOCK <= LANES
    cw = ccmp.shape[1]
    ovl = _overlap_padded(T // CMP_STRIDE, cw)
    skv_p = jnp.pad(skvb.reshape(B, T, KV_W), ((0, 0), (KEY_PAD, 0), (0, 0)))
    wkv_p = jnp.pad(wkvb.reshape(B, T, KV_W), ((0, 0), (WINDOW, 0), (0, 0)))
    rowblk = lambda w: pl.BlockSpec((Q_BLOCK, w), lambda b, i: (b * n_qb + i, 0))
    seq = lambda a: pl.BlockSpec((1,) + a.shape[1:], lambda b, i: (b, 0, 0))
    return pl.pallas_call(
        _nsa_body,
        grid=(B, n_qb),
        in_specs=[rowblk(Q_EXT), rowblk(_MISC_W), seq(ccmp), seq(skv_p), seq(wkv_p),
                  _full_spec(ovl), _full_spec(tc), _full_spec(m0), _full_spec(m1)],
        out_specs=rowblk(Q_EXT),
        out_shape=jax.ShapeDtypeStruct((B * T, Q_EXT), BF16),
        compiler_params=pltpu.CompilerParams(dimension_semantics=("parallel", "arbitrary"),
                                             vmem_limit_bytes=VMEM_LIMIT),
        name="nsa_prompt",
    )(nq, misc, ccmp, skv_p, wkv_p, ovl, tc, m0, m1)


_ORDER_OF_NEG_INF = -2 ** 31 + 0x7FFFFF


def _ordered_float(k):
    return lax.bitcast_convert_type(k ^ ((k >> 31) & 0x7FFFFFFF), F32)


def _kth_largest(count_ge, k, rows):
    zero = jnp.zeros((rows, 1), I32)
    ans = jnp.where(count_ge(_ordered_float(zero)) >= k, zero, jnp.full((rows, 1), INT_MIN, I32))

    def step(i, ans):
        cand = ans | jnp.left_shift(jnp.int32(1), 30 - i)
        return jnp.where(count_ge(_ordered_float(cand)) >= k, cand, ans)

    ans = lax.fori_loop(0, 31, step, ans)
    return jnp.where(ans > _ORDER_OF_NEG_INF, _ordered_float(ans), NEG_INF)


def _tie_cut(count_eq_below, r, n_bits, rows):
    def step(i, c):
        cand = c | jnp.left_shift(jnp.int32(1), n_bits - 1 - i)
        return jnp.where(count_eq_below(cand) < r, cand, c)

    return lax.fori_loop(0, n_bits, step, jnp.zeros((rows, 1), I32))


def _kept(s, col, floor, cut):
    return (s > floor) | ((s == floor) & (col <= cut))


def _topk_floor(count, min_score, last_col, k, n_bits, rows):
    kf = float(k)
    thr = _kth_largest(lambda cand: count(lambda s, col: s >= cand), kf, rows)
    finite = thr > NEG_INF
    n_gt = count(lambda s, col: s > thr)
    n_ge = count(lambda s, col: s >= thr)
    surplus = jnp.where(finite, n_ge - kf, 0.0)
    no_cut = jnp.full((rows, 1), 2 ** 31 - 1, I32)
    some_left = lambda left: jnp.max(left) > 0.0

    def by_search():
        return thr, _tie_cut(lambda c: count(lambda s, col: (s == thr) & (col < c)), kf - n_gt, n_bits, rows)

    def by_tie_drop():
        def drop(c):
            cut, left = c
            last = last_col(lambda s, col: (s == thr) & (col <= cut))
            return jnp.where(left > 0.0, last.astype(I32) - 1, cut), left - 1.0

        return thr, lax.while_loop(lambda c: some_left(c[1]), drop, (no_cut, surplus))[0]

    def by_value_drop():
        def drop(c):
            floor, cut, left = c
            worst = min_score(lambda s, col: _kept(s, col, floor, cut))
            last = last_col(lambda s, col: _kept(s, col, floor, cut) & (s == worst))
            go = left > 0.0
            return jnp.where(go, worst, floor), jnp.where(go, last.astype(I32) - 1, cut), left - 1.0

        return lax.while_loop(lambda c: some_left(c[2]), drop, (thr, no_cut, surplus))[:2]

    all_ties = jnp.min(jnp.where(surplus > 0.0, (n_ge - n_gt) - surplus, 1.0)) >= 1.0
    floor, cut = lax.cond(all_ties,
                          lambda: lax.cond(jnp.max(surplus) > 6.0, by_search, by_tie_drop),
                          by_value_drop)
    return floor, jnp.where(finite, cut, -1)


def _dsa_body(k_top, q_ref, iq_ref, misc_ref, ik_ref, dkv_ref, m0_ref, m1_ref, o_ref, key_scr):
    qb = pl.program_id(1)
    t0 = qb * Q_BLOCK
    tiles_per = KEY_TILE // Q_BLOCK
    n_tiles = qb // tiles_per + 1
    n_bits = int(key_scr.shape[1]).bit_length()
    row_t = lax.broadcasted_iota(I32, (Q_BLOCK, KEY_TILE), 0)
    col_t = lax.broadcasted_iota(I32, (Q_BLOCK, KEY_TILE), 1)

    def tile_row0(m):
        return pl.multiple_of(Q_BLOCK * (qb - tiles_per * m), Q_BLOCK)

    iq = jnp.concatenate([iq_ref[:, j * LANES:(j + 1) * LANES] for j in range(IDX_HEADS)], axis=0)
    iw = (misc_ref[:, _IW_COL:_IW_COL + IDX_HEADS] * IDX_HEADS ** -0.5).astype(BF16).astype(F32)

    def score_tile(m, _):
        row0 = tile_row0(m)
        x = _dot_nt(iq, ik_ref[0, pl.ds(row0, KEY_TILE), :])
        x = jnp.maximum(x, 0.0).astype(BF16).astype(F32)
        sc = x[0:Q_BLOCK] * iw[:, 0:1]
        for j in range(1, IDX_HEADS):
            sc = sc + x[j * Q_BLOCK:(j + 1) * Q_BLOCK] * iw[:, j:j + 1]
        pos = row0 - KEY_PAD + col_t
        valid = (pos >= 0) & (pos <= t0 + row_t)
        key_scr[:, pl.ds(row0, KEY_TILE)] = jnp.where(valid, jnp.where(sc == 0.0, 0.0, sc), NEG_INF)
        return 0

    lax.fori_loop(0, n_tiles, score_tile, 0)

    def fold_tiles(fn, init, combine):
        def body(m, acc):
            x = fn(key_scr[:, pl.ds(tile_row0(m), KEY_TILE)], tile_row0(m) + col_t)
            return combine(acc, combine(combine(x[:, 0:LANES], x[:, LANES:2 * LANES]),
                                        combine(x[:, 2 * LANES:3 * LANES], x[:, 3 * LANES:])))

        return lax.fori_loop(0, n_tiles, body, jnp.full((Q_BLOCK, LANES), init, F32))

    def count(pred):
        cnt = fold_tiles(lambda s, col: jnp.where(pred(s, col), 1.0, 0.0), 0.0, jnp.add)
        return jnp.sum(cnt, axis=-1, keepdims=True)

    def last_col(pred):
        last = fold_tiles(lambda s, col: jnp.where(pred(s, col), col.astype(F32), -1.0), -1.0, jnp.maximum)
        return jnp.max(last, axis=-1, keepdims=True)

    def min_score(pred):
        least = fold_tiles(lambda s, col: jnp.where(pred(s, col), s, jnp.inf), jnp.inf, jnp.minimum)
        return jnp.min(least, axis=-1, keepdims=True)

    floor, cut = _topk_floor(count, min_score, last_col, k_top, n_bits, Q_BLOCK)

    qs = [_stack_heads(q_ref, kv) for kv in range(DSA_KV_HEADS)]

    def attn_tile(m, carry, near):
        row0 = tile_row0(m)
        keep = _kept(key_scr[:, pl.ds(row0, KEY_TILE)], row0 + col_t, floor, cut)
        out = []
        for kv in range(DSA_KV_HEADS):
            sc = _dot_nt(qs[kv], dkv_ref[0, pl.ds(row0, KEY_TILE), 0:HALF])
            if near:
                sc = _add_near_bias(sc, kv, m0_ref, m1_ref)
            sc = _mask_rows(keep, sc)
            out.append(_online_update(carry[kv], sc, dkv_ref[0, pl.ds(row0, KEY_TILE), HALF:KV_W]))
        return tuple(out)

    carry = attn_tile(0, tuple(_online_init(GROUP * Q_BLOCK) for _ in range(DSA_KV_HEADS)), True)
    carry = lax.fori_loop(1, n_tiles, lambda m, c: attn_tile(m, c, False), carry)
    for kv in range(DSA_KV_HEADS):
        o = _online_finish(carry[kv])
        lane_keep = (lax.broadcasted_iota(I32, (Q_BLOCK, LANES), 1) // HEAD_DIM == kv).astype(F32)
        for g in range(GROUP):
            a = kv * GROUP + g
            o_ref[:, a * LANES:(a + 1) * LANES] = (o[g * Q_BLOCK:(g + 1) * Q_BLOCK] * lane_keep).astype(BF16)


def _dsa_prompt(dq, iq, misc, ikx, dkvb, tables, B, T):
    m0, m1, _ = tables
    n_qb = T // Q_BLOCK
    k_top = min(DSA_TOPK_MAX, T // 4)
    ik_p = jnp.pad(ikx.reshape(B, T, LANES), ((0, 0), (KEY_PAD, 0), (0, 0)))
    dkv_p = jnp.pad(dkvb.reshape(B, T, KV_W), ((0, 0), (KEY_PAD, 0), (0, 0)))
    rowblk = lambda w: pl.BlockSpec((Q_BLOCK, w), lambda b, i: (b * n_qb + i, 0))
    seq = lambda a: pl.BlockSpec((1,) + a.shape[1:], lambda b, i: (b, 0, 0))
    return pl.pallas_call(
        functools.partial(_dsa_body, k_top),
        grid=(B, n_qb),
        in_specs=[rowblk(Q_EXT), rowblk(IDX_HEADS * LANES), rowblk(_MISC_W), seq(ik_p), seq(dkv_p),
                  _full_spec(m0), _full_spec(m1)],
        out_specs=rowblk(Q_EXT),
        out_shape=jax.ShapeDtypeStruct((B * T, Q_EXT), BF16),
        scratch_shapes=[pltpu.VMEM((Q_BLOCK, T + KEY_PAD), F32)],
        compiler_params=pltpu.CompilerParams(dimension_semantics=("parallel", "arbitrary"),
                                             vmem_limit_bytes=VMEM_LIMIT),
        name="dsa_prompt",
    )(dq, iq, misc, ik_p, dkv_p, m0, m1)


def _ext_rows(w):
    d = w.shape[1]
    w4 = w.reshape(N_HEADS, 1, HEAD_DIM, d)
    sel = jnp.asarray(np.eye(2, dtype=np.float32)[[a // GROUP for a in range(N_HEADS)]], w.dtype)
    return (w4 * sel[:, :, None, None]).reshape(Q_EXT, d)


def _merge_body(x_ref, on_ref, od_ref, g_ref, wmg_ref, wa_ref, wb_ref, wo_ref, out_ref):
    x = x_ref[...]
    h = _rms(x, g_ref[...]).astype(BF16)
    m = jax.nn.sigmoid(_dot(h, wmg_ref[:, :D_MODEL])) * _dot(on_ref[...], wa_ref[...])
    m = m + jax.nn.sigmoid(_dot(h, wmg_ref[:, D_MODEL:])) * _dot(od_ref[...], wb_ref[...])
    out_ref[...] = x + _dot(m.astype(BF16), wo_ref[...])


def _merge(x2d, o_nsa, o_dsa, g, w_mg, w_a, w_b, w_o):
    rows = x2d.shape[0]
    tm = min(ROW_TILE, rows)
    row = lambda w: pl.BlockSpec((tm, w), lambda i: (i, 0))
    return pl.pallas_call(
        _merge_body,
        grid=(rows // tm,),
        in_specs=[row(D_MODEL), row(Q_EXT), row(Q_EXT), _full_spec(g), _full_spec(w_mg), _full_spec(w_a),
                  _full_spec(w_b), _full_spec(w_o)],
        out_specs=row(D_MODEL),
        out_shape=jax.ShapeDtypeStruct((rows, D_MODEL), F32),
        compiler_params=pltpu.CompilerParams(dimension_semantics=("parallel",), vmem_limit_bytes=VMEM_LIMIT),
        name="merge",
    )(x2d, o_nsa, o_dsa, g, w_mg, w_a, w_b, w_o)


def _ffn_body(n_chunks, x_ref, g_ref, gf_ref, wg_ref, wu_ref, wd_ref, out_ref):
    x = x_ref[...]
    h = _rms(x, g_ref[...]).astype(BF16)
    fc = wg_ref.shape[1] // n_chunks
    acc = x
    for c in range(n_chunks):
        gate = _dot(h, wg_ref[:, c * fc:(c + 1) * fc])
        up = _dot(h, wu_ref[:, c * fc:(c + 1) * fc])
        act = (gate * jax.nn.sigmoid(gate) * up).astype(BF16)
        acc = acc + _dot(act, wd_ref[c * fc:(c + 1) * fc, :])
    out_ref[...] = _rms(acc, gf_ref[...])


def _ffn(x2d, g, g_final, w_gate, w_up, w_down):
    rows = x2d.shape[0]
    tm = min(ROW_TILE, rows)
    d_ff = w_gate.shape[1]
    n_chunks = 2 if d_ff % (2 * LANES) == 0 else 1
    row = lambda w: pl.BlockSpec((tm, w), lambda i: (i, 0))
    return pl.pallas_call(
        functools.partial(_ffn_body, n_chunks),
        grid=(rows // tm,),
        in_specs=[row(D_MODEL), _full_spec(g), _full_spec(g_final), _full_spec(w_gate), _full_spec(w_up),
                  _full_spec(w_down)],
        out_specs=row(D_MODEL),
        out_shape=jax.ShapeDtypeStruct((rows, D_MODEL), F32),
        compiler_params=pltpu.CompilerParams(dimension_semantics=("parallel",), vmem_limit_bytes=VMEM_LIMIT),
        name="ffn",
    )(x2d, g, g_final, w_gate, w_up, w_down)


S_ROWS = 16


def _page_copy(pool_ref, page, buf_ref, slot, p, sem_ref):
    rows, width = pool_ref.shape[1:]
    if buf_ref.shape[2] == width:
        dst = buf_ref.at[slot, pl.ds(p * rows, rows)]
    else:
        dst = buf_ref.at[slot, :, pl.ds(p * width, width)]
    return pltpu.make_async_copy(pool_ref.at[page], dst, sem_ref.at[slot])


def _paged_step(pt_ref, pools):
    b = pl.program_id(0)
    n_pages = pt_ref.shape[1]
    slot = b % 2

    def fetch(seq, dst):
        for pool_ref, buf_ref, sem_ref in pools:
            for p in range(n_pages):
                _page_copy(pool_ref, pt_ref[seq, p], buf_ref, dst, p, sem_ref).start()

    @pl.when(b == 0)
    def _():
        fetch(0, 0)

    @pl.when(b + 1 < pl.num_programs(0))
    def _():
        fetch(b + 1, 1 - slot)

    for pool_ref, buf_ref, sem_ref in pools:
        for p in range(n_pages):
            _page_copy(pool_ref, 0, buf_ref, slot, p, sem_ref).wait()
    return slot


def _sum_over_heads(x):
    x = x + pltpu.roll(x, S_ROWS // 2, axis=0)
    return x + pltpu.roll(x, S_ROWS // 4, axis=0)


def _lane_keep(kv, rows):
    return (lax.broadcasted_iota(I32, (rows, LANES), 1) // HEAD_DIM == kv).astype(F32)


def _group_lane_keep():
    shape = (2 * S_ROWS, LANES)
    return (lax.broadcasted_iota(I32, shape, 1) // HEAD_DIM == lax.broadcasted_iota(I32, shape, 0) // S_ROWS).astype(F32)


def _sample_cmp_body(pt_ref, pool_ref, q_ref, pex_ref, w1_ref, w2_ref, b1_ref, bias_ref, ovl_ref, selval_ref,
                     ocmp_ref, sel_ref, buf, sem, rows_scr):
    slot = _paged_step(pt_ref, [(pool_ref, buf, sem)])
    n_tok = rows_scr.shape[1]
    page = pool_ref.shape[2]

    unroll = 4

    def to_rows(i, _):
        for u in range(unroll):
            tok = pl.ds(pl.multiple_of((i * unroll + u) * page, page), page)
            for c in range(2):
                rows_scr[c, tok, :] = buf[slot, c * HALF:(c + 1) * HALF, tok].T
        return 0

    lax.fori_loop(0, n_tok // (page * unroll), to_rows, 0)
    nc = n_tok // CMP_STRIDE
    chunk_cols = lambda l, c: rows_scr[c, pl.ds(l, nc, stride=CMP_STRIDE), :]
    o0, o1 = _compress_chunks(chunk_cols, nc, pex_ref[...], w1_ref, w2_ref, b1_ref)
    ck2 = o0.astype(BF16)
    cv2 = o1.astype(BF16)
    sv = selval_ref[...]
    for kv in range(NSA_KV_HEADS):
        pb = _softmax_rows(_dot_nt(q_ref[0, kv], ck2) + bias_ref[kv]).astype(BF16)
        ocmp_ref[0, kv] = _dot(pb, cv2)
        imp = _sum_over_heads(_dot(pb, ovl_ref[...]))
        sel = _select_topk(jnp.where(sv == 0.0, imp, sv), SEL_TOPK)
        sel_ref[0, kv] = jnp.where(sv == NEG_INF, 0.0, sel)


def _sample_sel_body(pt_ref, pool_ref, q_ref, selm_ref, ocmp_ref, ng_ref, snew_ref, swin_ref, wnew_ref,
                     sbias_ref, snbias_ref, wbias_ref, wnbias_ref, e_ref, enew_ref, o_ref, buf, sem):
    slot = _paged_step(pt_ref, [(pool_ref, buf, sem)])
    kt = buf[slot, 0:HALF, :].astype(BF16)
    vt = buf[slot, HALF:KV_W, :].astype(BF16)
    kn = snew_ref[0, :, 0:HALF].astype(BF16)
    vn = snew_ref[0, :, HALF:KV_W].astype(BF16)
    kwt = swin_ref[0, 0:HALF, :].astype(BF16)
    vwt = swin_ref[0, HALF:KV_W, :].astype(BF16)
    kwn = wnew_ref[0, :, 0:HALF].astype(BF16)
    vwn = wnew_ref[0, :, HALF:KV_W].astype(BF16)
    q = q_ref[0]
    selm = selm_ref[0].astype(BF16)
    tok = _dot(selm, e_ref[...]) > 0.5
    tokn = _dot(selm, enew_ref[...]) > 0.5
    p, pn = _softmax_pair(jnp.where(tok, _dot(q, kt) + sbias_ref[...], NEG_INF),
                          jnp.where(tokn, _dot_nt(q, kn) + snbias_ref[...], NEG_INF))
    o_sel = _dot_nt(p.astype(BF16), vt) + _dot(pn.astype(BF16), vn)
    pw, pwn = _softmax_pair(_dot(q, kwt) + wbias_ref[...], _dot_nt(q, kwn) + wnbias_ref[...])
    o_win = _dot_nt(pw.astype(BF16), vwt) + _dot(pwn.astype(BF16), vwn)
    g = jax.nn.sigmoid(ng_ref[0])
    o = g[:, 0:1] * ocmp_ref[0] + g[:, 1:2] * o_sel + g[:, 2:3] * o_win
    o_ref[0] = (o * _group_lane_keep()).astype(BF16)


def _sample_dsa_body(k_top, pt_ref, ipool_ref, kpool_ref, iq_ref, iw_ref, q_ref, inew_ref, knew_ref,
                     dbias_ref, dnbias_ref, newvalid_ref, o_ref, ibuf, kbuf, isem, ksem):
    slot = _paged_step(pt_ref, [(ipool_ref, ibuf, isem), (kpool_ref, kbuf, ksem)])
    n_keys = ibuf.shape[2]
    iq = iq_ref[0]
    iw = iw_ref[0].astype(BF16).astype(F32)

    def index_score(x):
        x = jnp.maximum(x, 0.0).astype(BF16).astype(F32)
        sc = _sum_over_heads(x * iw)
        return jnp.where(sc == 0.0, 0.0, sc)

    sc_c = index_score(_dot(iq, ibuf[slot].astype(BF16)))
    sc_n = jnp.where(newvalid_ref[...] == 0.0, index_score(_dot_nt(iq, inew_ref[0].astype(BF16))), NEG_INF)
    half = S_ROWS // 2
    hc, hn = sc_c[0:half], sc_n[0:half]
    col_c = lax.broadcasted_iota(I32, hc.shape, 1)
    col_n = n_keys + lax.broadcasted_iota(I32, hn.shape, 1)

    def reducer(where_fn, reduce_fn, combine):
        def run(pred):
            one = lambda s, col: reduce_fn(where_fn(pred(s, col), s, col), axis=-1, keepdims=True)
            return combine(one(hc, col_c), one(hn, col_n))
        return run

    count = reducer(lambda hit, s, col: jnp.where(hit, 1.0, 0.0), jnp.sum, jnp.add)
    min_score = reducer(lambda hit, s, col: jnp.where(hit, s, jnp.inf), jnp.min, jnp.minimum)
    last_col = reducer(lambda hit, s, col: jnp.where(hit, col.astype(F32), -1.0), jnp.max, jnp.maximum)
    floor, cut = _topk_floor(count, min_score, last_col, k_top, int(n_keys + S_ROWS).bit_length(), half)
    floor, cut = (jnp.concatenate([x, x], axis=0) for x in (floor, cut))
    keep_c = _kept(sc_c, lax.broadcasted_iota(I32, sc_c.shape, 1), floor, cut)
    keep_n = _kept(sc_n, n_keys + lax.broadcasted_iota(I32, sc_n.shape, 1), floor, cut)

    kt = kbuf[slot, 0:HALF, :].astype(BF16)
    vt = kbuf[slot, HALF:KV_W, :].astype(BF16)
    kn = knew_ref[0, :, 0:HALF].astype(BF16)
    vn = knew_ref[0, :, HALF:KV_W].astype(BF16)
    def masked(keep, s):
        n = s.shape[1]
        return jnp.where(keep[None], s.reshape(DSA_KV_HEADS, S_ROWS, n), NEG_INF).reshape(DSA_KV_HEADS * S_ROWS, n)

    q = q_ref[0]
    p, pn = _softmax_pair(masked(keep_c, _dot(q, kt) + dbias_ref[...]), masked(keep_n, _dot_nt(q, kn) + dnbias_ref[...]))
    o = _dot_nt(p.astype(BF16), vt) + _dot(pn.astype(BF16), vn)
    o_ref[0] = (o * _group_lane_keep()).astype(BF16)


def _sample_tables(tab_nsa, tab_dsa, past, dt, w_buf):
    n_chunks = past // CMP_STRIDE
    n_cmp = n_chunks - CMP_RATIO + 1
    qpos = past + (np.arange(S_ROWS) % dt)
    t_of = (np.arange(S_ROWS) % dt)[:, None]

    def per_head(tab, dist, valid):
        idx = _np_bucket(dist)
        far_col = np.all((idx == N_BUCKETS - 1) | ~valid, axis=0)
        n_far = int(np.argmin(far_col)) if not far_col.all() else idx.shape[1]
        out = []
        for kv in range(2):
            rows = []
            for g in range(GROUP):
                a = kv * GROUP + g
                near = tab[idx[g * dt:(g + 1) * dt, n_far:], a]
                rows.append(jnp.concatenate([jnp.broadcast_to(tab[N_BUCKETS - 1, a], (dt, n_far)), near], axis=1))
            out.append(jnp.concatenate(rows, axis=0))
        return jnp.where(jnp.asarray(valid)[None], jnp.stack(out), NEG_INF)

    n = np.arange(n_chunks)[None, :]
    d = qpos[:, None] - (CMP_STRIDE * n + CMP_BLOCK - 1)
    cmp_bias = per_head(tab_nsa, d, (d >= 0) & (n < n_cmp))
    s = np.arange(past)[None, :]
    d = qpos[:, None] - s
    sel_bias = per_head(tab_nsa, d, d >= 0)
    dsa_bias = per_head(tab_dsa, d, d >= 0)
    u = np.arange(S_ROWS)[None, :]
    new_ok = (u <= t_of) & (u < dt)
    new_bias_nsa = per_head(tab_nsa, t_of - u, new_ok)
    new_bias_dsa = per_head(tab_dsa, t_of - u, new_ok)
    i = np.arange(w_buf)[None, :]
    d = qpos[:, None] - (past - w_buf + i)
    win_bias = per_head(tab_nsa, d, (d >= 0) & (d < WINDOW) & (past - w_buf + i >= 0))
    new_valid = np.where(new_ok, 0.0, np.where(u < dt, -np.inf, 1.0)).astype(np.float32)

    n_sel = -(-(past + dt) // SEL_BLOCK)
    sel_w = -(-n_sel // LANES) * LANES
    j = np.arange(sel_w)[None, :]
    cur = (qpos // SEL_BLOCK)[:, None]
    forced = (j == 0) | (j == cur) | (j == cur - 1)
    sel_val = np.where(forced, np.inf, np.where((j <= cur) & (j < n_sel), 0.0, -np.inf)).astype(np.float32)
    nn = np.arange(n_chunks)[:, None]
    ovl = ((nn < n_cmp) & (nn * CMP_STRIDE < (j + 1) * SEL_BLOCK) & (nn * CMP_STRIDE + CMP_BLOCK > j * SEL_BLOCK))
    jj = np.arange(sel_w)[:, None]
    expand = jj == (np.arange(past)[None, :] // SEL_BLOCK)
    expand_new = (jj == ((past + u) // SEL_BLOCK)) & (u < dt)
    b16 = lambda m: jnp.asarray(m.astype(np.float32), BF16)
    return dict(cmp_bias=cmp_bias, sel_bias=sel_bias, dsa_bias=dsa_bias, new_bias_nsa=new_bias_nsa,
                new_bias_dsa=new_bias_dsa, win_bias=win_bias, new_valid=jnp.asarray(new_valid),
                sel_val=jnp.asarray(sel_val), ovl=b16(ovl), expand=b16(expand), expand_new=b16(expand_new))


def _sample_mixers(ps, DB, DT, c_cmp, c_sel, c_dkv, c_idx, s_win, page_table, cw, tab_nsa, tab_dsa):
    w1, w2, pex, b1 = cw
    n_pool, page = c_cmp.shape[:2]
    n_pages = page_table.shape[1]
    past = n_pages * page
    w_buf = s_win.shape[1]
    assert DT * GROUP == S_ROWS and past % KEY_TILE == 0
    tb = _sample_tables(tab_nsa, tab_dsa, past, DT, w_buf)
    k_top = min(DSA_TOPK_MAX, (past + DT) // 4)
    sel_w = tb["sel_val"].shape[1]

    rows = lambda a, w: jnp.transpose(a.reshape(DB, DT, 2, GROUP, w), (0, 2, 3, 1, 4)).reshape(DB, 2, S_ROWS, w)
    nq = rows(ps["nq"], LANES)
    dq = rows(ps["dq"], LANES)
    ng = rows(ps["misc"][:, _NG_COL:_NG_COL + 3 * NSA_HEADS], 3)
    iq = jnp.transpose(ps["iq"].reshape(DB, DT, IDX_HEADS, LANES)[..., :IDX_DIM], (0, 2, 1, 3)).reshape(
        DB, S_ROWS, IDX_DIM)
    iw = jnp.transpose(ps["misc"][:, _IW_COL:_IW_COL + IDX_HEADS].reshape(DB, DT, IDX_HEADS) * IDX_HEADS ** -0.5,
                       (0, 2, 1)).reshape(DB, S_ROWS, 1)
    new16 = lambda a: jnp.pad(a.reshape(DB, DT, a.shape[-1]), ((0, 0), (0, S_ROWS - DT), (0, 0)))
    s_new, w_new, d_new, i_new = new16(ps["skv"]), new16(ps["wkv"]), new16(ps["dkv"]), new16(ps["ik"])

    seq = lambda a: pl.BlockSpec((1,) + a.shape[1:], lambda b, pt: (b,) + (0,) * (a.ndim - 1))
    const = lambda a: pl.BlockSpec(a.shape, lambda b, pt: (0,) * a.ndim)
    hbm = pl.BlockSpec(memory_space=pl.ANY)
    params = pltpu.CompilerParams(dimension_semantics=("arbitrary",), vmem_limit_bytes=VMEM_LIMIT)
    grp = lambda w, dt: jax.ShapeDtypeStruct((DB, 2, S_ROWS, w), dt)
    out_grp = lambda w: pl.BlockSpec((1, 2, S_ROWS, w), lambda b, pt: (b, 0, 0, 0))

    token_minor = lambda a: jnp.moveaxis(a, 1, -1).reshape(a.shape[0], -1, a.shape[1])
    cmp_pool = token_minor(c_cmp)
    ins = [nq, pex, w1, w2, b1, tb["cmp_bias"], tb["ovl"], tb["sel_val"]]
    o_cmp, selm = pl.pallas_call(
        _sample_cmp_body,
        grid_spec=pltpu.PrefetchScalarGridSpec(
            num_scalar_prefetch=1, grid=(DB,),
            in_specs=[hbm, seq(nq)] + [const(a) for a in ins[1:]],
            out_specs=[out_grp(LANES), out_grp(sel_w)],
            scratch_shapes=[pltpu.VMEM((2, KV_W, past), F32), pltpu.SemaphoreType.DMA((2,)),
                            pltpu.VMEM((2, past, HALF), F32)]),
        out_shape=[grp(LANES, F32), grp(sel_w, F32)],
        compiler_params=params, name="sample_cmp",
    )(page_table, cmp_pool, *ins)

    sel_pool = token_minor(c_sel)
    s_win3 = token_minor(s_win)
    both = lambda a: a.reshape((a.shape[0], 2 * S_ROWS) + a.shape[3:])
    both_t = lambda a: a.reshape((2 * S_ROWS,) + a.shape[2:])
    out_both = pl.BlockSpec((1, 2 * S_ROWS, LANES), lambda b, pt: (b, 0, 0))
    both_shape = jax.ShapeDtypeStruct((DB, 2 * S_ROWS, LANES), BF16)
    ins = [both(nq), both(selm), both(o_cmp), both(ng), s_new, s_win3, w_new]
    consts = [both_t(tb["sel_bias"]), both_t(tb["new_bias_nsa"]), both_t(tb["win_bias"]), both_t(tb["new_bias_nsa"]),
              tb["expand"], tb["expand_new"]]
    o_nsa = pl.pallas_call(
        _sample_sel_body,
        grid_spec=pltpu.PrefetchScalarGridSpec(
            num_scalar_prefetch=1, grid=(DB,),
            in_specs=[hbm] + [seq(a) for a in ins] + [const(a) for a in consts],
            out_specs=out_both,
            scratch_shapes=[pltpu.VMEM((2, KV_W, past), F32), pltpu.SemaphoreType.DMA((2,))]),
        out_shape=both_shape,
        compiler_params=params, name="sample_sel",
    )(page_table, sel_pool, *ins, *consts)

    idx_pool = token_minor(c_idx)
    dkv_pool = token_minor(c_dkv)
    ins = [iq, iw, both(dq), i_new, d_new]
    consts = [both_t(tb["dsa_bias"]), both_t(tb["new_bias_dsa"]), tb["new_valid"]]
    o_dsa = pl.pallas_call(
        functools.partial(_sample_dsa_body, k_top),
        grid_spec=pltpu.PrefetchScalarGridSpec(
            num_scalar_prefetch=1, grid=(DB,),
            in_specs=[hbm, hbm] + [seq(a) for a in ins] + [const(a) for a in consts],
            out_specs=out_both,
            scratch_shapes=[pltpu.VMEM((2, IDX_DIM, past), F32), pltpu.VMEM((2, KV_W, past), F32),
                            pltpu.SemaphoreType.DMA((2,)), pltpu.SemaphoreType.DMA((2,))]),
        out_shape=both_shape,
        compiler_params=params, name="sample_dsa",
    )(page_table, idx_pool, dkv_pool, *ins, *consts)

    unrows = lambda o: jnp.transpose(o.reshape(DB, 2, GROUP, DT, LANES), (0, 3, 1, 2, 4)).reshape(DB * DT, Q_EXT)
    return unrows(o_nsa), unrows(o_dsa)


def kernel(x_prompt, x_sample, cache_nsa_cmp, cache_nsa_sel, cache_dsa_kv, cache_dsa_idx, state_nsa_win, page_table, norm_mix, w_in, cmp_pe, cmp_w1, cmp_b1, cmp_w2, rel_bias, w_branch_nsa, w_branch_dsa, w_out, norm_ffn, w_gate, w_up, w_down, norm_final):
    assert norm_mix.shape[0] == 1, "single trunk layer"
    B, T, _ = x_prompt.shape
    DB, DT, _ = x_sample.shape
    tab_nsa = rel_bias[:, :NSA_HEADS]
    tab_dsa = rel_bias[:, NSA_HEADS:]
    l = 0
    g_mix = norm_mix[l][None]
    g_ffn = norm_ffn[l][None]
    g_fin = norm_final[None]
    w_pack, w_mg = _pack_w_in(w_in[l])
    w_a = _ext_rows(w_branch_nsa[l]).astype(BF16)
    w_b = _ext_rows(w_branch_dsa[l]).astype(BF16)
    w_o = w_out[l].astype(BF16)
    wg, wu, wd = (w.astype(BF16) for w in (w_gate[l], w_up[l], w_down[l]))
    cw = _pack_cmp_weights(cmp_pe[l], cmp_w1[l], cmp_b1[l], cmp_w2[l])

    def tail(x2d, o_n, o_d):
        x1 = _merge(x2d, o_n, o_d, g_mix, w_mg, w_a, w_b, w_o)
        return _ffn(x1, g_ffn, g_fin, wg, wu, wd)

    xp = x_prompt.reshape(B * T, D_MODEL)
    pp = _project(xp, g_mix, w_pack)
    ccmp = _compress_prompt(pp["ckv"].reshape(B, T, KV_W), cw)
    o_n = _nsa_prompt(pp["nq"], pp["misc"], ccmp, pp["skvb"], pp["wkvb"], _prompt_tables(tab_nsa), B, T)
    o_d = _dsa_prompt(pp["dq"], pp["iq"], pp["misc"], pp["ikx"], pp["dkvb"], _prompt_tables(tab_dsa), B, T)
    y_prompt = tail(xp, o_n, o_d).reshape(B, T, D_MODEL)

    xs = x_sample.reshape(DB * DT, D_MODEL)
    ps = _project(xs, g_mix, w_pack)
    o_n, o_d = _sample_mixers(ps, DB, DT, cache_nsa_cmp[l], cache_nsa_sel[l], cache_dsa_kv[l], cache_dsa_idx[l],
                              state_nsa_win[l], page_table, cw, tab_nsa, tab_dsa)
    y_sample = tail(xs, o_n, o_d).reshape(DB, DT, D_MODEL)

    kv6 = lambda a, b, t: a.reshape(1, b, t, 2, NSA_KV_HEADS, HEAD_DIM)
    win_s = jnp.concatenate([state_nsa_win[l], kv6(ps["wkv"], DB, DT)[0].astype(state_nsa_win.dtype)], axis=1)[:, DT:]
    w_keep = min(WINDOW, T)
    return (y_prompt, y_sample,
            kv6(pp["ckv"], B, T), kv6(ps["ckv"], DB, DT),
            kv6(pp["skv"], B, T), kv6(ps["skv"], DB, DT),
            kv6(pp["dkv"], B, T), kv6(ps["dkv"], DB, DT),
            pp["ik"].reshape(1, B, T, IDX_DIM), ps["ik"].reshape(1, DB, DT, IDX_DIM),
            kv6(pp["wkv"], B, T)[:, :, T - w_keep:], win_s[None])
```
